```python
import math
import jax
import jax.numpy as jnp
from jax import lax
import numpy as np


D_MODEL = 2048
BATCH = 2
SEQ = 16384
DEPTH = 2
DEC_BATCH = 2
DEC_SEQ = 4096
PAST_LEN = 128

HEAD_DIM = 128
Q_BLOCK = 128
A_PATTERNS = ((128, 1), (512, 4), (2048, 16))
A_KV_HEADS = 4
A_Q_HEADS = len(A_PATTERNS) * A_KV_HEADS
ROPE_THETA = 500000.0
PARTIAL_ROT = HEAD_DIM // 4
B_HEADS = 8
B_Q_RANK = 512
B_KV_RANK = 256
B_NOPE = 128
B_ROPE = 64
B_V = 128
C_Q_HEADS = 8
C_KV_HEADS = 2
AXIAL_THETA = 10000.0
GRID_W = 64
N_BRANCH = 3
IN_SPLIT_SIZES = (A_Q_HEADS * HEAD_DIM, A_KV_HEADS * HEAD_DIM, A_KV_HEADS * HEAD_DIM,
                  B_Q_RANK, B_KV_RANK, B_ROPE,
                  C_Q_HEADS * HEAD_DIM, C_KV_HEADS * HEAD_DIM, C_KV_HEADS * HEAD_DIM,
                  N_BRANCH * D_MODEL)
IN_COLS = sum(IN_SPLIT_SIZES)
N_EXPERTS = 64
TOP_K = 8
N_EXPERT_GROUPS = 8
TOPK_GROUPS = 4
EXPERT_FF = 512
SHARED_FF = 512
ROUTED_SCALE = 2.5
MOE_BLOCK = 128
ALPHA = (2 * DEPTH) ** 0.25
BETA = (8 * DEPTH) ** -0.25
LN_EPS = 1e-5
RMS_EPS = 1e-6
MASK_VALUE = -1e30

kernel_name = 'hybrid_dilated_mla_axial_moe_encoder'


def layer_norm(x, g, b):
    xf = x.astype(jnp.float32)
    mu = jnp.mean(xf, -1, keepdims=True)
    var = jnp.mean(jnp.square(xf - mu), -1, keepdims=True)
    y = (xf - mu) * lax.rsqrt(var + LN_EPS) * g.astype(jnp.float32) + b.astype(jnp.float32)
    return y.astype(x.dtype)


def rms_norm(x, g):
    xf = x.astype(jnp.float32)
    y = xf * lax.rsqrt(jnp.mean(jnp.square(xf), -1, keepdims=True) + RMS_EPS) * g.astype(jnp.float32)
    return y.astype(x.dtype)


def rope_table(pos, dim, theta):
    inv = theta ** (-jnp.arange(0, dim, 2, dtype=jnp.float32) / dim)
    ang = pos.astype(jnp.float32)[:, None] * inv[None, :]
    return jnp.cos(ang), jnp.sin(ang)


def apply_rope(x, cos, sin):
    half = cos.shape[-1]
    xf = x.astype(jnp.float32)
    x1, x2 = xf[..., :half], xf[..., half:]
    c = cos[None, :, None, :]
    s = sin[None, :, None, :]
    return jnp.concatenate([x1 * c - x2 * s, x1 * s + x2 * c], -1).astype(x.dtype)


def rope_prefix(x, cos, sin):
    r = 2 * cos.shape[-1]
    return jnp.concatenate([apply_rope(x[..., :r], cos, sin), x[..., r:]], -1)


def axial_rope(x, cos_r, sin_r, cos_c, sin_c):
    half = x.shape[-1] // 2
    return jnp.concatenate([apply_rope(x[..., :half], cos_r, sin_r),
                            apply_rope(x[..., half:], cos_c, sin_c)], -1)


def split_cols(h, sizes):
    offsets = np.cumsum(np.array(sizes))[:-1].tolist()
    return jnp.split(h, offsets, axis=-1)


def swiglu(h, wg, wu, wd):
    return (jax.nn.silu(h @ wg) * (h @ wu)) @ wd


def dense_attention(q, k, v, scale):
    b, s, hk, g, dq = q.shape
    nb = s // Q_BLOCK
    qb = q.reshape(b, nb, Q_BLOCK, hk, g, dq).transpose(1, 0, 2, 3, 4, 5)

    def block(qi):
        sc = jnp.einsum('bqhgd,bkhd->bhgqk', qi, k).astype(jnp.float32) * scale
        p = jax.nn.softmax(sc, axis=-1)
        return jnp.einsum('bhgqk,bkhd->bqhgd', p.astype(v.dtype), v)

    o = lax.map(block, qb)
    return o.transpose(1, 0, 2, 3, 4, 5).reshape(b, s, hk, g, v.shape[-1])


def dilated_window_attention(q, k, v, window, dilation):
    b, s, h, e = q.shape
    r = (window // dilation) // 2
    length = s // dilation
    nb = -(-length // Q_BLOCK)
    lp = nb * Q_BLOCK
    kw_len = Q_BLOCK + 2 * r
    z = b * dilation

    def phases(t):
        return t.reshape(b, length, dilation, h, e).transpose(0, 2, 1, 3, 4).reshape(z, length, h, e)

    qp = jnp.pad(phases(q), ((0, 0), (0, lp - length), (0, 0), (0, 0)))
    kp = jnp.pad(phases(k), ((0, 0), (r, lp - length + r), (0, 0), (0, 0)))
    vp = jnp.pad(phases(v), ((0, 0), (r, lp - length + r), (0, 0), (0, 0)))
    win = (jnp.arange(nb) * Q_BLOCK)[:, None] + jnp.arange(kw_len)[None, :]
    kwin = kp[:, win]
    vwin = vp[:, win]
    qb = qp.reshape(z, nb, Q_BLOCK, h, e)
    sc = jnp.einsum('znqhe,znkhe->znhqk', qb, kwin).astype(jnp.float32) * (e ** -0.5)
    qpos = (jnp.arange(nb) * Q_BLOCK)[:, None] + jnp.arange(Q_BLOCK)[None, :]
    kpos = win - r
    valid = ((jnp.abs(qpos[:, :, None] - kpos[:, None, :]) <= r)
             & (kpos[:, None, :] >= 0) & (kpos[:, None, :] < length))
    sc = jnp.where(valid[None, :, None], sc, MASK_VALUE)
    m = jnp.max(sc, -1, keepdims=True)
    p = jnp.exp(sc - m)
    den = jnp.sum(p, -1, keepdims=True)
    o = jnp.einsum('znhqk,znkhe->znqhe', (p / den).astype(v.dtype), vwin)
    lse = (m + jnp.log(den))[..., 0].transpose(0, 1, 3, 2)
    o = o.reshape(z, lp, h, e)[:, :length].reshape(b, dilation, length, h, e)
    o = o.transpose(0, 2, 1, 3, 4).reshape(b, s, h, e)
    lse = lse.reshape(z, lp, h)[:, :length].reshape(b, dilation, length, h)
    lse = lse.transpose(0, 2, 1, 3).reshape(b, s, h)
    return o, lse


def token_mixer(x, tabs, w_in, mla_q_norm, w_uq, mla_kv_norm, w_ukv, c_q_norm, c_k_norm,
                w_o_a, w_o_b, w_o_c, w_out):
    b, s, _ = x.shape
    cos_a, sin_a, cos_b, sin_b, cos_r, sin_r, cos_c, sin_c = tabs
    h = x @ w_in
    a_q, a_k, a_v, b_cq, b_ckv, b_kr, c_q, c_k, c_v, gate = split_cols(h, IN_SPLIT_SIZES)

    a_q = rope_prefix(a_q.reshape(b, s, A_Q_HEADS, HEAD_DIM), cos_a, sin_a)
    a_k = rope_prefix(a_k.reshape(b, s, A_KV_HEADS, HEAD_DIM), cos_a, sin_a)
    a_v = a_v.reshape(b, s, A_KV_HEADS, HEAD_DIM)
    outs, lses = [], []
    for gi, (window, dilation) in enumerate(A_PATTERNS):
        o_g, l_g = dilated_window_attention(a_q[:, :, gi * A_KV_HEADS:(gi + 1) * A_KV_HEADS],
                                            a_k, a_v, window, dilation)
        outs.append(o_g)
        lses.append(l_g)
    wts = jax.nn.softmax(jnp.stack(lses), axis=0)
    o_a = jnp.einsum('gbsh,gbshe->bshe', wts, jnp.stack(outs).astype(jnp.float32))
    o_a = o_a.astype(x.dtype).reshape(b, s, A_KV_HEADS * HEAD_DIM)

    cq = rms_norm(b_cq, mla_q_norm)
    qb = (cq @ w_uq).reshape(b, s, B_HEADS, B_NOPE + B_ROPE)
    qb = jnp.concatenate([qb[..., :B_NOPE], apply_rope(qb[..., B_NOPE:], cos_b, sin_b)], -1)
    ckv = rms_norm(b_ckv, mla_kv_norm)
    kv = (ckv @ w_ukv).reshape(b, s, B_HEADS, B_NOPE + B_V)
    kr = apply_rope(b_kr.reshape(b, s, 1, B_ROPE), cos_b, sin_b)
    kb = jnp.concatenate([kv[..., :B_NOPE], jnp.broadcast_to(kr, (b, s, B_HEADS, B_ROPE))], -1)
    vb = kv[..., B_NOPE:]
    o_b = dense_attention(qb[:, :, :, None, :], kb, vb, (B_NOPE + B_ROPE) ** -0.5)
    o_b = o_b.reshape(b, s, B_HEADS * B_V)

    qc = axial_rope(rms_norm(c_q.reshape(b, s, C_Q_HEADS, HEAD_DIM), c_q_norm), cos_r, sin_r, cos_c, sin_c)
    kc = axial_rope(rms_norm(c_k.reshape(b, s, C_KV_HEADS, HEAD_DIM), c_k_norm), cos_r, sin_r, cos_c, sin_c)
    vc = c_v.reshape(b, s, C_KV_HEADS, HEAD_DIM)
    o_c = dense_attention(qc.reshape(b, s, C_KV_HEADS, C_Q_HEADS // C_KV_HEADS, HEAD_DIM), kc, vc,
                          HEAD_DIM ** -0.5)
    o_c = o_c.reshape(b, s, C_Q_HEADS * HEAD_DIM)

    g = jax.nn.sigmoid(gate.astype(jnp.float32)).astype(x.dtype).reshape(b, s, N_BRANCH, D_MODEL)
    merged = g[:, :, 0] * (o_a @ w_o_a) + g[:, :, 1] * (o_b @ w_o_b) + g[:, :, 2] * (o_c @ w_o_c)
    return merged @ w_out


def route(xf, w_router, router_bias):
    n = xf.shape[0]
    scores = jax.nn.sigmoid((xf @ w_router).astype(jnp.float32))
    biased = scores + router_bias.astype(jnp.float32)
    grp = biased.reshape(n, N_EXPERT_GROUPS, N_EXPERTS // N_EXPERT_GROUPS)
    grp_score = jnp.sum(lax.top_k(grp, 2)[0], -1)
    _, gidx = lax.top_k(grp_score, TOPK_GROUPS)
    gmask = jnp.sum(jax.nn.one_hot(gidx, N_EXPERT_GROUPS, dtype=jnp.float32), 1) > 0
    emask = jnp.repeat(gmask, N_EXPERTS // N_EXPERT_GROUPS, axis=1)
    _, idx = lax.top_k(jnp.where(emask, biased, -jnp.inf), TOP_K)
    w = jnp.take_along_axis(scores, idx, axis=1)
    w = w / jnp.sum(w, -1, keepdims=True) * ROUTED_SCALE
    return idx, w


def routed_experts(xf, idx, wts, w_e_gate, w_e_up, w_e_down):
    n, d = xf.shape
    e = w_e_gate.shape[0]
    a = n * TOP_K
    flat_e = idx.reshape(a)
    flat_t = jnp.arange(a, dtype=jnp.int32) // TOP_K
    flat_w = wts.reshape(a)
    order = jnp.argsort(flat_e)
    se = flat_e[order]
    counts = jnp.bincount(flat_e, length=e)
    padded = (counts + MOE_BLOCK - 1) // MOE_BLOCK * MOE_BLOCK
    pad_end = jnp.cumsum(padded)
    pad_start = pad_end - padded
    start = jnp.cumsum(counts) - counts
    dest = pad_start[se] + (jnp.arange(a, dtype=jnp.int32) - start[se])
    n_blocks = -(-a // MOE_BLOCK) + e
    p_len = n_blocks * MOE_BLOCK
    buf_t = jnp.full((p_len,), n, jnp.int32).at[dest].set(flat_t[order])
    buf_w = jnp.zeros((p_len,), xf.dtype).at[dest].set(flat_w[order].astype(xf.dtype))
    blk_e = jnp.minimum(jnp.searchsorted(pad_end, jnp.arange(n_blocks, dtype=jnp.int32) * MOE_BLOCK,
                                         side='right'), e - 1)
    xpad = jnp.concatenate([xf, jnp.zeros((1, d), xf.dtype)], 0)

    def step(acc, inp):
        t, w, ei = inp
        y = swiglu(xpad[t], w_e_gate[ei], w_e_up[ei], w_e_down[ei]) * w[:, None]
        return acc.at[t].add(y), None

    acc, _ = lax.scan(step, jnp.zeros((n + 1, d), xf.dtype),
                      (buf_t.reshape(n_blocks, MOE_BLOCK), buf_w.reshape(n_blocks, MOE_BLOCK), blk_e))
    return acc[:n]


def moe(x, w_router, router_bias, w_e_gate, w_e_up, w_e_down, w_s_gate, w_s_up, w_s_down):
    b, s, d = x.shape
    xf = x.reshape(b * s, d)
    idx, wts = route(xf, w_router, router_bias)
    y = swiglu(xf, w_s_gate, w_s_up, w_s_down) + routed_experts(xf, idx, wts, w_e_gate, w_e_up, w_e_down)
    return y.reshape(b, s, d)


def run_trunk(x, ln1_g, ln1_b, w_in, mla_q_norm, w_uq, mla_kv_norm, w_ukv, c_q_norm, c_k_norm,
              w_o_a, w_o_b, w_o_c, w_out, ln2_g, ln2_b, w_router, router_bias,
              w_e_gate, w_e_up, w_e_down, w_s_gate, w_s_up, w_s_down):
    s = x.shape[1]
    pos = jnp.arange(s)
    rows = s // GRID_W
    row = jnp.repeat(jnp.arange(rows), GRID_W)
    col = pos % GRID_W
    cos_a, sin_a = rope_table(pos, PARTIAL_ROT, ROPE_THETA)
    cos_b, sin_b = rope_table(pos, B_ROPE, ROPE_THETA)
    cos_r, sin_r = rope_table(row, HEAD_DIM // 2, AXIAL_THETA)
    cos_c, sin_c = rope_table(col, HEAD_DIM // 2, AXIAL_THETA)
    tabs = (cos_a, sin_a, cos_b, sin_b, cos_r, sin_r, cos_c, sin_c)
    for l in range(DEPTH):
        mix = token_mixer(x, tabs, w_in[l], mla_q_norm[l], w_uq[l], mla_kv_norm[l], w_ukv[l],
                          c_q_norm[l], c_k_norm[l], w_o_a[l], w_o_b[l], w_o_c[l], w_out[l])
        x = layer_norm(ALPHA * x + mix, ln1_g[l], ln1_b[l])
        ffn = moe(x, w_router[l], router_bias[l], w_e_gate[l], w_e_up[l], w_e_down[l],
                  w_s_gate[l], w_s_up[l], w_s_down[l])
        x = layer_norm(ALPHA * x + ffn, ln2_g[l], ln2_b[l])
    return x


def setup_inputs(seed: int = 0) -> dict:
    key = jax.random.key(seed)
    ks = jax.random.split(key, 32)

    def nrm(k, shape, scale):
        return jax.random.normal(k, shape, jnp.float32) * scale

    def gain(k, n):
        return 1.0 + 0.02 * jax.random.normal(k, (DEPTH, n), jnp.float32)

    return {
        'x_prompt': nrm(ks[0], (BATCH, SEQ, D_MODEL), 1.0),
        'x_sample': nrm(ks[1], (DEC_BATCH, DEC_SEQ, D_MODEL), 1.0),
        'ln1_g': gain(ks[2], D_MODEL),
        'ln1_b': nrm(ks[3], (DEPTH, D_MODEL), 0.02),
        'w_in': nrm(ks[4], (DEPTH, D_MODEL, IN_COLS), D_MODEL ** -0.5),
        'mla_q_norm': gain(ks[5], B_Q_RANK),
        'w_uq': nrm(ks[6], (DEPTH, B_Q_RANK, B_HEADS * (B_NOPE + B_ROPE)), B_Q_RANK ** -0.5),
        'mla_kv_norm': gain(ks[7], B_KV_RANK),
        'w_ukv': nrm(ks[8], (DEPTH, B_KV_RANK, B_HEADS * (B_NOPE + B_V)), B_KV_RANK ** -0.5),
        'c_q_norm': gain(ks[9], HEAD_DIM),
        'c_k_norm': gain(ks[10], HEAD_DIM),
        'w_o_a': nrm(ks[11], (DEPTH, A_KV_HEADS * HEAD_DIM, D_MODEL), (A_KV_HEADS * HEAD_DIM) ** -0.5),
        'w_o_b': nrm(ks[12], (DEPTH, B_HEADS * B_V, D_MODEL), (B_HEADS * B_V) ** -0.5),
        'w_o_c': nrm(ks[13], (DEPTH, C_Q_HEADS * HEAD_DIM, D_MODEL), (C_Q_HEADS * HEAD_DIM) ** -0.5),
        'w_out': nrm(ks[14], (DEPTH, D_MODEL, D_MODEL), D_MODEL ** -0.5 * BETA),
        'ln2_g': gain(ks[15], D_MODEL),
        'ln2_b': nrm(ks[16], (DEPTH, D_MODEL), 0.02),
        'w_router': nrm(ks[17], (DEPTH, D_MODEL, N_EXPERTS), D_MODEL ** -0.5),
        'router_bias': nrm(ks[18], (DEPTH, N_EXPERTS), 0.01),
        'w_e_gate': nrm(ks[19], (DEPTH, N_EXPERTS, D_MODEL, EXPERT_FF), D_MODEL ** -0.5),
        'w_e_up': nrm(ks[20], (DEPTH, N_EXPERTS, D_MODEL, EXPERT_FF), D_MODEL ** -0.5),
        'w_e_down': nrm(ks[21], (DEPTH, N_EXPERTS, EXPERT_FF, D_MODEL), EXPERT_FF ** -0.5 * BETA),
        'w_s_gate': nrm(ks[22], (DEPTH, D_MODEL, SHARED_FF), D_MODEL ** -0.5),
        'w_s_up': nrm(ks[23], (DEPTH, D_MODEL, SHARED_FF), D_MODEL ** -0.5),
        'w_s_down': nrm(ks[24], (DEPTH, SHARED_FF, D_MODEL), SHARED_FF ** -0.5 * BETA),
    }


def reference(x_prompt, x_sample, ln1_g, ln1_b, w_in, mla_q_norm, w_uq, mla_kv_norm, w_ukv,
              c_q_norm, c_k_norm, w_o_a, w_o_b, w_o_c, w_out, ln2_g, ln2_b, w_router, router_bias,
              w_e_gate, w_e_up, w_e_down, w_s_gate, w_s_up, w_s_down):
    y_prompt = run_trunk(x_prompt, ln1_g, ln1_b, w_in, mla_q_norm, w_uq, mla_kv_norm, w_ukv,
                         c_q_norm, c_k_norm, w_o_a, w_o_b, w_o_c, w_out, ln2_g, ln2_b, w_router,
                         router_bias, w_e_gate, w_e_up, w_e_down, w_s_gate, w_s_up, w_s_down)
    y_sample = run_trunk(x_sample, ln1_g, ln1_b, w_in, mla_q_norm, w_uq, mla_kv_norm, w_ukv,
                         c_q_norm, c_k_norm, w_o_a, w_o_b, w_o_c, w_out, ln2_g, ln2_b, w_router,
                         router_bias, w_e_gate, w_e_up, w_e_down, w_s_gate, w_s_up, w_s_down)
    return (y_prompt, y_sample)
```

```python
import functools

import numpy as np
import jax
import jax.numpy as jnp
from jax import lax
from jax.experimental import pallas as pl
from jax.experimental.pallas import tpu as pltpu

F32 = jnp.float32
BF16 = jnp.bfloat16

D_MODEL = 2048
HEAD_DIM = 128
A_PATTERNS = ((128, 1), (512, 4), (2048, 16))
A_KV_HEADS = 4
A_Q_HEADS = len(A_PATTERNS) * A_KV_HEADS
ROPE_THETA = 500000.0
PARTIAL_ROT = HEAD_DIM // 4
B_HEADS = 8
B_Q_RANK = 512
B_KV_RANK = 256
B_NOPE = 128
B_ROPE = 64
B_V = 128
C_Q_HEADS = 8
C_KV_HEADS = 2
AXIAL_THETA = 10000.0
GRID_W = 64
N_BRANCH = 3
N_EXPERTS = 64
TOP_K = 8
N_EXPERT_GROUPS = 8
TOPK_GROUPS = 4
EXPERT_FF = 512
SHARED_FF = 512
ROUTED_SCALE = 2.5
LN_EPS = 1e-5
RMS_EPS = 1e-6
MASK_VALUE = -1e30

LANES = 128
VMEM_LIMIT = 56 * 1024 * 1024

GATE_OFF = 0
AQ_OFF = GATE_OFF + N_BRANCH * D_MODEL
AK_OFF = AQ_OFF + A_Q_HEADS * HEAD_DIM
AV_OFF = AK_OFF + A_KV_HEADS * HEAD_DIM
BCQ_OFF = AV_OFF + A_KV_HEADS * HEAD_DIM
BCKV_OFF = BCQ_OFF + B_Q_RANK
BKR_OFF = BCKV_OFF + B_KV_RANK
CQ_OFF = BKR_OFF + LANES
CK_OFF = CQ_OFF + C_Q_HEADS * HEAD_DIM
CV_OFF = CK_OFF + C_KV_HEADS * HEAD_DIM
H_USED = CV_OFF + C_KV_HEADS * HEAD_DIM
H_COLS = -(-H_USED // 1024) * 1024

MOE_ROWS = 256


def _tile(n, t):
    t = min(n, t)
    assert n % t == 0, (n, t)
    return t


def _params(sem):
    return pltpu.CompilerParams(dimension_semantics=sem, vmem_limit_bytes=VMEM_LIMIT)


def _resident(shape):
    nd = len(shape)
    return pl.BlockSpec(shape, lambda *_: (0,) * nd, pipeline_mode=pl.Buffered(1))


def _mm_body(x_ref, w_ref, o_ref):
    o_ref[...] = jnp.dot(x_ref[...], w_ref[...], preferred_element_type=F32).astype(o_ref.dtype)


def _mm_rms_body(x_ref, g_ref, w_ref, o_ref):
    xf = x_ref[...].astype(F32)
    y = xf * lax.rsqrt(jnp.mean(xf * xf, -1, keepdims=True) + RMS_EPS) * g_ref[...]
    o_ref[...] = jnp.dot(y.astype(BF16), w_ref[...], preferred_element_type=F32).astype(o_ref.dtype)


def matmul(x, w, *, x_col=0, gain=None, tm=1024, tn=1024, out_dtype=BF16):
    m = x.shape[0]
    k, n = w.shape
    tm, tn = _tile(m, tm), _tile(n, tn)
    assert x_col % k == 0
    x_spec = pl.BlockSpec((tm, k), lambda i, j: (i, x_col // k))
    w_spec = pl.BlockSpec((k, tn), lambda i, j: (0, j))
    o_spec = pl.BlockSpec((tm, tn), lambda i, j: (i, j))
    if gain is None:
        body, specs, args = _mm_body, [x_spec, w_spec], (x, w)
    else:
        g_spec = pl.BlockSpec((1, k), lambda i, j: (0, 0))
        body, specs, args = _mm_rms_body, [x_spec, g_spec, w_spec], (x, gain.reshape(1, k).astype(F32), w)
    return pl.pallas_call(
        body, grid=(m // tm, n // tn), in_specs=specs, out_specs=o_spec,
        out_shape=jax.ShapeDtypeStruct((m, n), out_dtype),
        compiler_params=_params(("parallel", "arbitrary")))(*args)


def _rope_body(*refs, shift, rms):
    if rms:
        x_ref, c_ref, s1_ref, s2_ref, g_ref, o_ref = refs
    else:
        x_ref, c_ref, s1_ref, s2_ref, o_ref = refs
    x = x_ref[...].astype(F32)
    if rms:
        x = x * lax.rsqrt(jnp.mean(x * x, -1, keepdims=True) + RMS_EPS) * g_ref[...]
    out = x * c_ref[...] + pltpu.roll(x, LANES - shift, 1) * s1_ref[...] + pltpu.roll(x, shift, 1) * s2_ref[...]
    o_ref[...] = out.astype(o_ref.dtype)


def rope(x, tabs, *, batch, seq, col, heads, shift, gains=None, ts=1024):
    ts = _tile(seq, ts)
    ns = seq // ts
    blk = col // LANES
    x_spec = pl.BlockSpec((ts, LANES), lambda b, s, h: (b * ns + s, blk + h))
    t_spec = pl.BlockSpec((ts, LANES), lambda b, s, h: (s, 0))
    o_spec = pl.BlockSpec((ts, LANES), lambda b, s, h: (b * ns + s, h))
    specs, args = [x_spec, t_spec, t_spec, t_spec], [x, *tabs]
    if gains is not None:
        specs.append(pl.BlockSpec((None, 1, LANES), lambda b, s, h: (h, 0, 0)))
        args.append(gains.reshape(heads, 1, LANES).astype(F32))
    return pl.pallas_call(
        functools.partial(_rope_body, shift=shift, rms=gains is not None),
        grid=(batch, ns, heads), in_specs=specs, out_specs=o_spec,
        out_shape=jax.ShapeDtypeStruct((batch * seq, heads * LANES), BF16),
        compiler_params=_params(("parallel", "arbitrary", "arbitrary")))(*args)


def _rope_tables(seq):
    def table(pos, dim, theta):
        inv = theta ** (-jnp.arange(0, dim, 2, dtype=F32) / dim)
        ang = pos.astype(F32)[:, None] * inv[None, :]
        return jnp.cos(ang), jnp.sin(ang)

    pos = jnp.arange(seq)
    row = pos // GRID_W
    colp = pos % GRID_W
    one = lambda n: jnp.ones((seq, n), F32)
    zero = lambda n: jnp.zeros((seq, n), F32)
    cat = lambda *xs: jnp.concatenate(xs, -1)
    ca, sa = table(pos, PARTIAL_ROT, ROPE_THETA)
    ra = LANES - PARTIAL_ROT
    tab_a = (cat(ca, ca, one(ra)), cat(-sa, zero(LANES - 16)), cat(zero(16), sa, zero(ra)))
    cb, sb = table(pos, B_ROPE, ROPE_THETA)
    tab_b = (cat(cb, cb, one(64)), cat(-sb, zero(96)), cat(zero(32), sb, zero(64)))
    cr, sr = table(row, HEAD_DIM // 2, AXIAL_THETA)
    cc, sc = table(colp, HEAD_DIM // 2, AXIAL_THETA)
    tab_c = (cat(cr, cr, cc, cc), cat(-sr, zero(32), -sc, zero(32)), cat(zero(32), sr, zero(32), sc))
    return tab_a, tab_b, tab_c


def _flash_body(*refs, nq, nk, group, tq, tkc, seq, scale):
    q_refs, k_refs = refs[:nq], refs[nq:nq + nk]
    v_ref, o_ref, m_scr, l_scr, acc_scr = refs[nq + nk:]
    if group > 1:
        q = jnp.concatenate([q_refs[0][:, g * LANES:(g + 1) * LANES] for g in range(group)], axis=0)
    else:
        q = jnp.concatenate([r[...] for r in q_refs], axis=1)
    m_scr[...] = jnp.full(m_scr.shape, -jnp.inf, F32)
    l_scr[...] = jnp.zeros(l_scr.shape, F32)
    acc_scr[...] = jnp.zeros(acc_scr.shape, F32)

    def step(j, carry):
        off = pl.multiple_of(j * tkc, tkc)
        k = jnp.concatenate([r[pl.ds(off, tkc), :] for r in k_refs], axis=1)
        v = v_ref[pl.ds(off, tkc), :]
        s = lax.dot_general(q, k, (((1,), (1,)), ((), ())), preferred_element_type=F32) * scale
        m_prev = m_scr[...]
        m_new = jnp.maximum(m_prev, jnp.max(s, -1, keepdims=True))
        alpha = jnp.exp(m_prev - m_new)
        p = jnp.exp(s - m_new)
        l_scr[...] = alpha * l_scr[...] + jnp.sum(p, -1, keepdims=True)
        acc_scr[...] = alpha * acc_scr[...] + jnp.dot(p.astype(BF16), v, preferred_element_type=F32)
        m_scr[...] = m_new
        return carry

    lax.fori_loop(0, seq // tkc, step, 0)
    o = acc_scr[...] / l_scr[...]
    for g in range(group):
        o_ref[:, g * LANES:(g + 1) * LANES] = o[g * tq:(g + 1) * tq].astype(o_ref.dtype)


def flash_attention(q_srcs, k_srcs, v_src, *, batch, seq, kv_heads, group, scale, rows=1024, tkc=512):
    tq = _tile(seq, rows // group)
    tkc = _tile(seq, tkc)
    nqb = seq // tq
    nq, nk = len(q_srcs), len(k_srcs)
    qw = group * LANES
    specs, args = [], []
    for arr, col, stride in q_srcs:
        assert col % qw == 0 and (stride * 1) % qw == 0 or stride == 0
        specs.append(pl.BlockSpec((tq, qw), lambda b, h, i, c=col // qw, st=stride // qw: (b * nqb + i, c + h * st)))
        args.append(arr)
    for arr, col, stride in k_srcs:
        specs.append(pl.BlockSpec((seq, LANES), lambda b, h, i, c=col // LANES, st=stride // LANES: (b, c + h * st)))
        args.append(arr)
    arr, col = v_src
    specs.append(pl.BlockSpec((seq, LANES), lambda b, h, i, c=col // LANES: (b, c + h)))
    args.append(arr)
    r = group * tq
    return pl.pallas_call(
        functools.partial(_flash_body, nq=nq, nk=nk, group=group, tq=tq, tkc=tkc, seq=seq, scale=scale),
        grid=(batch, kv_heads, nqb), in_specs=specs,
        out_specs=pl.BlockSpec((tq, qw), lambda b, h, i: (b * nqb + i, h)),
        out_shape=jax.ShapeDtypeStruct((batch * seq, kv_heads * qw), BF16),
        scratch_shapes=[pltpu.VMEM((r, 1), F32), pltpu.VMEM((r, 1), F32), pltpu.VMEM((r, LANES), F32)],
        compiler_params=_params(("parallel", "arbitrary", "arbitrary")))(*args)


A_QB = 128
A_RADIUS = 64


def _dilated_body(q_ref, kp_ref, km_ref, kn_ref, vp_ref, vm_ref, vn_ref, o_ref, lse_ref, *, length, scale):
    n = pl.program_id(2)
    kwin = jnp.concatenate([kp_ref[...], km_ref[...], kn_ref[...]], axis=0)
    vwin = jnp.concatenate([vp_ref[...], vm_ref[...], vn_ref[...]], axis=0)
    kw = A_QB + 2 * A_RADIUS
    qi = lax.broadcasted_iota(jnp.int32, (A_QB, kw), 0)
    kj = lax.broadcasted_iota(jnp.int32, (A_QB, kw), 1)
    rel = kj - A_RADIUS - qi
    kpos = n * A_QB - A_RADIUS + kj
    bad = (jnp.abs(rel) > A_RADIUS).astype(jnp.int32) + (kpos < 0).astype(jnp.int32) + (kpos >= length).astype(jnp.int32)
    valid = bad == 0
    for h in range(A_KV_HEADS):
        sl = slice(h * LANES, (h + 1) * LANES)
        s = lax.dot_general(q_ref[:, sl], kwin[:, sl], (((1,), (1,)), ((), ())),
                            preferred_element_type=F32) * scale
        s = jnp.where(valid, s, MASK_VALUE)
        m = jnp.max(s, -1, keepdims=True)
        p = jnp.exp(s - m)
        den = jnp.sum(p, -1, keepdims=True)
        pn = (p / den).astype(BF16)
        o_ref[:, sl] = jnp.dot(pn, vwin[:, sl], preferred_element_type=F32)
        lse_ref[:, sl] = jnp.broadcast_to(m + jnp.log(den), (A_QB, LANES))


def dilated_attention(qk, h, *, batch, seq, group_idx, dilation):
    d = dilation
    length = seq // d
    assert length % A_QB == 0
    nl = length // A_QB
    qkc, hc = qk.shape[1], h.shape[1]
    gw = A_KV_HEADS * LANES
    qk_v = qk.reshape(batch * length, d * qkc)
    h_v = h.reshape(batch * length, d * hc)
    half = A_QB // 2

    def main(c):
        return lambda cols: (lambda b, p, n: (b * nl + n, (p * cols + c) // gw))

    def prev(c):
        return lambda cols: (lambda b, p, n: (jnp.maximum(2 * (b * nl + n) - 1, 2 * b * nl), (p * cols + c) // gw))

    def nxt(c):
        return lambda cols: (lambda b, p, n: (jnp.minimum(2 * (b * nl + n) + 2, 2 * (b + 1) * nl - 1), (p * cols + c) // gw))

    q_col = group_idx * gw
    k_col = A_Q_HEADS * LANES
    in_specs = [
        pl.BlockSpec((A_QB, gw), main(q_col)(qkc)),
        pl.BlockSpec((half, gw), prev(k_col)(qkc)),
        pl.BlockSpec((A_QB, gw), main(k_col)(qkc)),
        pl.BlockSpec((half, gw), nxt(k_col)(qkc)),
        pl.BlockSpec((half, gw), prev(AV_OFF)(hc)),
        pl.BlockSpec((A_QB, gw), main(AV_OFF)(hc)),
        pl.BlockSpec((half, gw), nxt(AV_OFF)(hc)),
    ]
    o_spec = pl.BlockSpec((A_QB, gw), lambda b, p, n: (b * nl + n, p))
    o, lse = pl.pallas_call(
        functools.partial(_dilated_body, length=length, scale=HEAD_DIM ** -0.5),
        grid=(batch, d, nl), in_specs=in_specs, out_specs=[o_spec, o_spec],
        out_shape=[jax.ShapeDtypeStruct((batch * length, d * gw), F32)] * 2,
        compiler_params=_params(("parallel", "arbitrary", "arbitrary")))(
            qk_v, qk_v, qk_v, qk_v, h_v, h_v, h_v)
    return o.reshape(batch * seq, gw), lse.reshape(batch * seq, gw)


def _sigmoid(x):
    return 1.0 / (1.0 + jnp.exp(-x))


def _layer_norm(y, g, b):
    mu = jnp.mean(y, -1, keepdims=True)
    yc = y - mu
    var = jnp.mean(yc * yc, -1, keepdims=True)
    return yc * lax.rsqrt(var + LN_EPS) * g + b


def _merge_body(oa0, oa1, oa2, l0, l1, l2, ob_ref, oc_ref, g0_ref, g1_ref, g2_ref,
                wa_ref, wb_ref, wc_ref, o_ref):
    la, lb, lc = l0[...], l1[...], l2[...]
    mx = jnp.maximum(jnp.maximum(la, lb), lc)
    ea, eb, ec = jnp.exp(la - mx), jnp.exp(lb - mx), jnp.exp(lc - mx)
    tot = ea + eb + ec
    o_a = (ea / tot) * oa0[...] + (eb / tot) * oa1[...] + (ec / tot) * oa2[...]
    ya = jnp.dot(o_a.astype(BF16), wa_ref[...], preferred_element_type=F32)
    yb = jnp.dot(ob_ref[...], wb_ref[...], preferred_element_type=F32)
    yc = jnp.dot(oc_ref[...], wc_ref[...], preferred_element_type=F32)
    merged = (_sigmoid(g0_ref[...].astype(F32)) * ya + _sigmoid(g1_ref[...].astype(F32)) * yb
              + _sigmoid(g2_ref[...].astype(F32)) * yc)
    o_ref[...] = merged.astype(o_ref.dtype)


def merge_mixers(oa, lse, ob, oc, h, w_o_a, w_o_b, w_o_c, *, tm=256):
    n = ob.shape[0]
    tm = _tile(n, tm)
    row = lambda w: pl.BlockSpec((tm, w), lambda i: (i, 0))
    gate = lambda j: pl.BlockSpec((tm, D_MODEL), lambda i: (i, GATE_OFF // D_MODEL + j))
    specs = ([row(oa[0].shape[1])] * 6 + [row(ob.shape[1]), row(oc.shape[1]), gate(0), gate(1), gate(2)]
             + [_resident(w_o_a.shape), _resident(w_o_b.shape), _resident(w_o_c.shape)])
    return pl.pallas_call(
        _merge_body, grid=(n // tm,), in_specs=specs, out_specs=row(D_MODEL),
        out_shape=jax.ShapeDtypeStruct((n, D_MODEL), BF16),
        compiler_params=_params(("parallel",)))(*oa, *lse, ob, oc, h, h, h, w_o_a, w_o_b, w_o_c)


def _proj_ln_body(a_ref, w_ref, x_ref, g_ref, b_ref, o_ref, ob_ref, *, alpha):
    y = alpha * x_ref[...] + jnp.dot(a_ref[...], w_ref[...], preferred_element_type=F32)
    out = _layer_norm(y, g_ref[...], b_ref[...])
    o_ref[...] = out
    ob_ref[...] = out.astype(BF16)


def proj_residual_ln(a, w, x, g, b, *, alpha, tm=256):
    n, d = x.shape
    tm = _tile(n, tm)
    row = lambda wd: pl.BlockSpec((tm, wd), lambda i: (i, 0))
    vec = pl.BlockSpec((1, d), lambda i: (0, 0))
    return pl.pallas_call(
        functools.partial(_proj_ln_body, alpha=alpha), grid=(n // tm,),
        in_specs=[row(a.shape[1]), _resident(w.shape), row(d), vec, vec],
        out_specs=[row(d), row(d)],
        out_shape=[jax.ShapeDtypeStruct((n, d), F32), jax.ShapeDtypeStruct((n, d), BF16)],
        compiler_params=_params(("parallel",)))(a, w, x, g.reshape(1, d), b.reshape(1, d))


def _split_bf16(x):
    hi = x.astype(BF16)
    lo = (x - hi.astype(F32)).astype(BF16)
    return hi, lo


def _router_body(x_ref, wh_ref, wl_ref, bias_ref, idx_ref, wt_ref):
    xh, xl = _split_bf16(x_ref[...])
    nt = (((1,), (1,)), ((), ()))
    wh, wl = wh_ref[...], wl_ref[...]
    logits = (lax.dot_general(wh, xh, nt, preferred_element_type=F32)
              + (lax.dot_general(wh, xl, nt, preferred_element_type=F32)
                 + lax.dot_general(wl, xh, nt, preferred_element_type=F32)))
    scores = _sigmoid(logits)
    biased = scores + bias_ref[...]
    tn = scores.shape[1]
    gsz = N_EXPERTS // N_EXPERT_GROUPS
    neg = -jnp.inf
    sub = lax.broadcasted_iota(jnp.int32, (gsz, tn), 0).astype(F32)
    gs = []
    for g in range(N_EXPERT_GROUPS):
        blk = biased[g * gsz:(g + 1) * gsz]
        m1 = jnp.max(blk, 0, keepdims=True)
        first = jnp.min(jnp.where(blk == m1, sub, float(gsz)), 0, keepdims=True)
        m2 = jnp.max(jnp.where(sub == first, neg, blk), 0, keepdims=True)
        gs.append(m1 + m2)
    gs = jnp.concatenate(gs, axis=0)
    gi = lax.broadcasted_iota(jnp.int32, (N_EXPERT_GROUPS, tn), 0).astype(F32)
    gsel = jnp.zeros((N_EXPERT_GROUPS, tn), F32)
    for _ in range(TOPK_GROUPS):
        gm = jnp.max(gs, 0, keepdims=True)
        pick = gi == jnp.min(jnp.where(gs == gm, gi, float(N_EXPERT_GROUPS)), 0, keepdims=True)
        gsel = jnp.where(pick, 1.0, gsel)
        gs = jnp.where(pick, neg, gs)
    masked = jnp.concatenate(
        [jnp.where(gsel[g:g + 1] > 0.0, biased[g * gsz:(g + 1) * gsz], neg) for g in range(N_EXPERT_GROUPS)], axis=0)
    ei = lax.broadcasted_iota(jnp.int32, (N_EXPERTS, tn), 0).astype(F32)
    idxs, wts = [], []
    for _ in range(TOP_K):
        mx = jnp.max(masked, 0, keepdims=True)
        ix = jnp.min(jnp.where(masked == mx, ei, float(N_EXPERTS)), 0, keepdims=True)
        pick = ei == ix
        idxs.append(ix)
        wts.append(jnp.sum(jnp.where(pick, scores, 0.0), 0, keepdims=True))
        masked = jnp.where(pick, neg, masked)
    w = jnp.concatenate(wts, axis=0)
    idx_ref[...] = jnp.concatenate(idxs, axis=0).astype(jnp.int32)
    wt_ref[...] = w / jnp.sum(w, 0, keepdims=True) * ROUTED_SCALE


def router(x, w_router, router_bias, *, tn=512):
    n, d = x.shape
    tn = _tile(n, tn)
    wt = w_router.T.astype(F32)
    wh = wt.astype(BF16)
    wl = (wt - wh.astype(F32)).astype(BF16)
    full = lambda r, c: pl.BlockSpec((r, c), lambda i: (0, 0))
    out = pl.BlockSpec((TOP_K, tn), lambda i: (0, i))
    return pl.pallas_call(
        _router_body, grid=(n // tn,),
        in_specs=[pl.BlockSpec((tn, d), lambda i: (i, 0)), full(N_EXPERTS, d), full(N_EXPERTS, d), full(N_EXPERTS, 1)],
        out_specs=[out, out],
        out_shape=[jax.ShapeDtypeStruct((TOP_K, n), jnp.int32), jax.ShapeDtypeStruct((TOP_K, n), F32)],
        compiler_params=_params(("parallel",)))(x, wh, wl, router_bias.reshape(N_EXPERTS, 1).astype(F32))


def _silu(x):
    return x * _sigmoid(x)


def _experts_body(blk_e_ref, nused_ref, x_ref, wgu_ref, wd_ref, o_ref):
    i = pl.program_id(0)

    @pl.when(i < nused_ref[0])
    def _():
        hgu = jnp.dot(x_ref[...], wgu_ref[...], preferred_element_type=F32)
        a = _silu(hgu[:, :EXPERT_FF]) * hgu[:, EXPERT_FF:]
        o_ref[...] = jnp.dot(a.astype(BF16), wd_ref[...], preferred_element_type=F32)

    @pl.when(i >= nused_ref[0])
    def _():
        o_ref[...] = jnp.zeros(o_ref.shape, o_ref.dtype)


def grouped_experts(xs, blk_e, nused, w_gu, w_d):
    p, d = xs.shape
    nb = p // MOE_ROWS
    grid_spec = pltpu.PrefetchScalarGridSpec(
        num_scalar_prefetch=2, grid=(nb,),
        in_specs=[pl.BlockSpec((MOE_ROWS, d), lambda i, e, u: (i, 0)),
                  pl.BlockSpec((None, d, 2 * EXPERT_FF), lambda i, e, u: (e[i], 0, 0)),
                  pl.BlockSpec((None, EXPERT_FF, d), lambda i, e, u: (e[i], 0, 0))],
        out_specs=pl.BlockSpec((MOE_ROWS, d), lambda i, e, u: (i, 0)))
    return pl.pallas_call(
        _experts_body, grid_spec=grid_spec, out_shape=jax.ShapeDtypeStruct((p, d), F32),
        compiler_params=_params(("arbitrary",)))(blk_e, nused, xs, w_gu, w_d)


def _shared_ln_body(x_ref, xb_ref, r_ref, wgu_ref, wd_ref, g_ref, b_ref, o_ref, ob_ref, *, alpha):
    hgu = jnp.dot(xb_ref[...], wgu_ref[...], preferred_element_type=F32)
    a = _silu(hgu[:, :SHARED_FF]) * hgu[:, SHARED_FF:]
    y = jnp.dot(a.astype(BF16), wd_ref[...], preferred_element_type=F32) + r_ref[...]
    out = _layer_norm(alpha * x_ref[...] + y, g_ref[...], b_ref[...])
    o_ref[...] = out
    ob_ref[...] = out.astype(BF16)


def shared_expert_ln(x, xb, routed, w_gu, w_d, g, b, *, alpha, tm=256):
    n, d = x.shape
    tm = _tile(n, tm)
    row = pl.BlockSpec((tm, d), lambda i: (i, 0))
    vec = pl.BlockSpec((1, d), lambda i: (0, 0))
    return pl.pallas_call(
        functools.partial(_shared_ln_body, alpha=alpha), grid=(n // tm,),
        in_specs=[row, row, row, _resident(w_gu.shape), _resident(w_d.shape), vec, vec],
        out_specs=[row, row],
        out_shape=[jax.ShapeDtypeStruct((n, d), F32), jax.ShapeDtypeStruct((n, d), BF16)],
        compiler_params=_params(("parallel",)))(x, xb, routed, w_gu, w_d, g.reshape(1, d), b.reshape(1, d))


def _dispatch_plan(idx, n):
    a = n * TOP_K
    flat_e = idx.T.reshape(a)
    flat_t = jnp.arange(a, dtype=jnp.int32) // TOP_K
    order = jnp.argsort(flat_e)
    se = flat_e[order]
    counts = jnp.bincount(flat_e, length=N_EXPERTS)
    padded = (counts + MOE_ROWS - 1) // MOE_ROWS * MOE_ROWS
    pad_end = jnp.cumsum(padded)
    pad_start = pad_end - padded
    start = jnp.cumsum(counts) - counts
    dest = (pad_start[se] + (jnp.arange(a, dtype=jnp.int32) - start[se])).astype(jnp.int32)
    n_blocks = a // MOE_ROWS + N_EXPERTS
    p_len = n_blocks * MOE_ROWS
    slot_token = jnp.zeros((p_len,), jnp.int32).at[dest].set(flat_t[order])
    blk_e = jnp.minimum(jnp.searchsorted(pad_end, jnp.arange(n_blocks, dtype=jnp.int32) * MOE_ROWS, side='right'),
                        N_EXPERTS - 1).astype(jnp.int32)
    nused = (pad_end[-1] // MOE_ROWS).astype(jnp.int32).reshape(1)
    slot_of = jnp.zeros((a,), jnp.int32).at[order].set(dest).reshape(n, TOP_K)
    return slot_token, blk_e, nused, slot_of


def _prep_layer(l, p):
    bf = lambda a: a.astype(BF16)
    w_in = p['w_in'][l]
    offs = np.cumsum((0, A_Q_HEADS * HEAD_DIM, A_KV_HEADS * HEAD_DIM, A_KV_HEADS * HEAD_DIM, B_Q_RANK, B_KV_RANK,
                      B_ROPE, C_Q_HEADS * HEAD_DIM, C_KV_HEADS * HEAD_DIM, C_KV_HEADS * HEAD_DIM, N_BRANCH * D_MODEL))
    piece = lambda i: w_in[:, offs[i]:offs[i + 1]]
    zeros = lambda c: jnp.zeros((D_MODEL, c), w_in.dtype)
    w_h = jnp.concatenate([piece(9), piece(0), piece(1), piece(2), piece(3), piece(4), piece(5), zeros(LANES - B_ROPE),
                           piece(6), piece(7), piece(8), zeros(H_COLS - H_USED)], axis=1)
    w_uq = p['w_uq'][l].reshape(B_Q_RANK, B_HEADS, B_NOPE + B_ROPE)
    w_uq_rope = jnp.pad(w_uq[:, :, B_NOPE:], ((0, 0), (0, 0), (0, LANES - B_ROPE)))
    w_q = jnp.concatenate([w_uq[:, :, :B_NOPE].reshape(B_Q_RANK, -1), w_uq_rope.reshape(B_Q_RANK, -1)], axis=1)
    w_ukv = p['w_ukv'][l].reshape(B_KV_RANK, B_HEADS, B_NOPE + B_V)
    w_kv = jnp.concatenate([w_ukv[:, :, :B_NOPE].reshape(B_KV_RANK, -1), w_ukv[:, :, B_NOPE:].reshape(B_KV_RANK, -1)], axis=1)
    return dict(
        w_h=bf(w_h), w_q=bf(w_q), w_kv=bf(w_kv),
        w_o_a=bf(p['w_o_a'][l]), w_o_b=bf(p['w_o_b'][l]), w_o_c=bf(p['w_o_c'][l]), w_out=bf(p['w_out'][l]),
        w_e_gu=bf(jnp.concatenate([p['w_e_gate'][l], p['w_e_up'][l]], axis=-1)), w_e_d=bf(p['w_e_down'][l]),
        w_s_gu=bf(jnp.concatenate([p['w_s_gate'][l], p['w_s_up'][l]], axis=-1)), w_s_d=bf(p['w_s_down'][l]),
        c_gains=jnp.concatenate([jnp.tile(p['c_q_norm'][l][None], (C_Q_HEADS, 1)),
                                 jnp.tile(p['c_k_norm'][l][None], (C_KV_HEADS, 1))], axis=0),
    )


def _token_mixer(x, xb, lw, l, p, tabs, batch, seq, alpha):
    tab_a, tab_b, tab_c = tabs
    h = matmul(xb, lw['w_h'])
    qk_a = rope(h, tab_a, batch=batch, seq=seq, col=AQ_OFF, heads=A_Q_HEADS + A_KV_HEADS, shift=PARTIAL_ROT // 2)
    oa, lse = [], []
    for gi, (window, dilation) in enumerate(A_PATTERNS):
        assert (window // dilation) // 2 == A_RADIUS
        o_g, l_g = dilated_attention(qk_a, h, batch=batch, seq=seq, group_idx=gi, dilation=dilation)
        oa.append(o_g)
        lse.append(l_g)
    q_b = matmul(h, lw['w_q'], x_col=BCQ_OFF, gain=p['mla_q_norm'][l])
    kv_b = matmul(h, lw['w_kv'], x_col=BCKV_OFF, gain=p['mla_kv_norm'][l])
    q_b_rope = rope(q_b, tab_b, batch=batch, seq=seq, col=B_HEADS * LANES, heads=B_HEADS, shift=B_ROPE // 2)
    k_b_rope = rope(h, tab_b, batch=batch, seq=seq, col=BKR_OFF, heads=1, shift=B_ROPE // 2)
    o_b = flash_attention([(q_b, 0, LANES), (q_b_rope, 0, LANES)], [(kv_b, 0, LANES), (k_b_rope, 0, 0)],
                          (kv_b, B_HEADS * LANES), batch=batch, seq=seq, kv_heads=B_HEADS, group=1,
                          scale=(B_NOPE + B_ROPE) ** -0.5)
    qk_c = rope(h, tab_c, batch=batch, seq=seq, col=CQ_OFF, heads=C_Q_HEADS + C_KV_HEADS, shift=HEAD_DIM // 4,
                gains=lw['c_gains'])
    group = C_Q_HEADS // C_KV_HEADS
    o_c = flash_attention([(qk_c, 0, group * LANES)], [(qk_c, C_Q_HEADS * LANES, LANES)], (h, CV_OFF),
                          batch=batch, seq=seq, kv_heads=C_KV_HEADS, group=group, scale=HEAD_DIM ** -0.5)
    merged = merge_mixers(oa, lse, o_b, o_c, h, lw['w_o_a'], lw['w_o_b'], lw['w_o_c'])
    return proj_residual_ln(merged, lw['w_out'], x, p['ln1_g'][l], p['ln1_b'][l], alpha=alpha)


def _moe(x, xb, lw, l, p, alpha):
    n = x.shape[0]
    idx, wts = router(x, p['w_router'][l], p['router_bias'][l])
    slot_token, blk_e, nused, slot_of = _dispatch_plan(idx, n)
    xs = jnp.take(xb, slot_token, axis=0)
    y = grouped_experts(xs, blk_e, nused, lw['w_e_gu'], lw['w_e_d'])
    routed = jnp.sum(jnp.take(y, slot_of, axis=0) * wts.T[:, :, None], axis=1)
    return shared_expert_ln(x, xb, routed, lw['w_s_gu'], lw['w_s_d'], p['ln2_g'][l], p['ln2_b'][l], alpha=alpha)


def _run_trunk(x3, layers, p):
    batch, seq, d = x3.shape
    depth = len(layers)
    alpha = (2 * depth) ** 0.25
    tabs = _rope_tables(seq)
    x = x3.reshape(batch * seq, d)
    xb = x.astype(BF16)
    for l, lw in enumerate(layers):
        x, xb = _token_mixer(x, xb, lw, l, p, tabs, batch, seq, alpha)
        x, xb = _moe(x, xb, lw, l, p, alpha)
    return x.reshape(batch, seq, d)


def kernel(x_prompt, x_sample, ln1_g, ln1_b, w_in, mla_q_norm, w_uq, mla_kv_norm, w_ukv, c_q_norm, c_k_norm, w_o_a, w_o_b, w_o_c, w_out, ln2_g, ln2_b, w_router, router_bias, w_e_gate, w_e_up, w_e_down, w_s_gate, w_s_up, w_s_down):
    p = dict(ln1_g=ln1_g, ln1_b=ln1_b, w_in=w_in, mla_q_norm=mla_q_norm, w_uq=w_uq, mla_kv_norm=mla_kv_norm,
             w_ukv=w_ukv, c_q_norm=c_q_norm, c_k_norm=c_k_norm, w_o_a=w_o_a, w_o_b=w_o_b, w_o_c=w_o_c, w_out=w_out,
             ln2_g=ln2_g, ln2_b=ln2_b, w_router=w_router, router_bias=router_bias, w_e_gate=w_e_gate,
             w_e_up=w_e_up, w_e_down=w_e_down, w_s_gate=w_s_gate, w_s_up=w_s_up, w_s_down=w_s_down)
    layers = [_prep_layer(l, p) for l in range(w_in.shape[0])]
    return _run_trunk(x_prompt, layers, p), _run_trunk(x_sample, layers, p)
```

```python
import functools

import numpy as np
import jax
import jax.numpy as jnp
from jax import lax
from jax.experimental import pallas as pl
from jax.experimental.pallas import tpu as pltpu

F32 = jnp.float32
BF16 = jnp.bfloat16

D_MODEL = 2048
HEAD_DIM = 128
A_PATTERNS = ((128, 1), (512, 4), (2048, 16))
A_KV_HEADS = 4
A_Q_HEADS = len(A_PATTERNS) * A_KV_HEADS
ROPE_THETA = 500000.0
PARTIAL_ROT = HEAD_DIM // 4
B_HEADS = 8
B_Q_RANK = 512
B_KV_RANK = 256
B_NOPE = 128
B_ROPE = 64
B_V = 128
C_Q_HEADS = 8
C_KV_HEADS = 2
AXIAL_THETA = 10000.0
GRID_W = 64
N_BRANCH = 3
N_EXPERTS = 64
TOP_K = 8
N_EXPERT_GROUPS = 8
TOPK_GROUPS = 4
EXPERT_FF = 512
SHARED_FF = 512
ROUTED_SCALE = 2.5
LN_EPS = 1e-5
RMS_EPS = 1e-6
MASK_VALUE = -1e30

LANES = 128
VMEM_LIMIT = 56 * 1024 * 1024
_NT = (((1,), (1,)), ((), ()))

GATE_OFF = 0
AQ_OFF = GATE_OFF + N_BRANCH * D_MODEL
AK_OFF = AQ_OFF + A_Q_HEADS * HEAD_DIM
AV_OFF = AK_OFF + A_KV_HEADS * HEAD_DIM
BCQ_OFF = AV_OFF + A_KV_HEADS * HEAD_DIM
BCKV_OFF = BCQ_OFF + B_Q_RANK
BKR_OFF = BCKV_OFF + B_KV_RANK
CQ_OFF = BKR_OFF + LANES
CK_OFF = CQ_OFF + C_Q_HEADS * HEAD_DIM
CV_OFF = CK_OFF + C_KV_HEADS * HEAD_DIM
H_USED = CV_OFF + C_KV_HEADS * HEAD_DIM
H_COLS = -(-H_USED // 1024) * 1024

MOE_ROWS = 256


def _tile(n, t):
    t = min(n, t)
    assert n % t == 0, (n, t)
    return t


def _params(sem):
    return pltpu.CompilerParams(dimension_semantics=sem, vmem_limit_bytes=VMEM_LIMIT)


def _resident(shape):
    nd = len(shape)
    return pl.BlockSpec(shape, lambda *_: (0,) * nd, pipeline_mode=pl.Buffered(1))


def _mm_body(x_ref, w_ref, o_ref):
    o_ref[...] = jnp.dot(x_ref[...], w_ref[...], preferred_element_type=F32).astype(o_ref.dtype)


def _mm_rms_body(x_ref, g_ref, w_ref, o_ref, *, out_scale):
    xf = x_ref[...].astype(F32)
    y = xf * lax.rsqrt(jnp.mean(xf * xf, -1, keepdims=True) + RMS_EPS) * g_ref[...]
    acc = jnp.dot(y.astype(BF16), w_ref[...], preferred_element_type=F32)
    o_ref[...] = (acc * out_scale).astype(o_ref.dtype)


def matmul(x, w, *, name, x_col=0, gain=None, out_scale=1.0, tm=1024, tn=1024, out_dtype=BF16):
    m = x.shape[0]
    k, n = w.shape
    tm, tn = _tile(m, tm), _tile(n, tn)
    assert x_col % k == 0
    x_spec = pl.BlockSpec((tm, k), lambda i, j: (i, x_col // k))
    w_spec = pl.BlockSpec((k, tn), lambda i, j: (0, j))
    o_spec = pl.BlockSpec((tm, tn), lambda i, j: (i, j))
    if gain is None:
        assert out_scale == 1.0
        body, specs, args = _mm_body, [x_spec, w_spec], (x, w)
    else:
        g_spec = pl.BlockSpec((1, k), lambda i, j: (0, 0))
        body = functools.partial(_mm_rms_body, out_scale=out_scale)
        specs, args = [x_spec, g_spec, w_spec], (x, gain.reshape(1, k).astype(F32), w)
    return pl.pallas_call(
        body, grid=(m // tm, n // tn), in_specs=specs, out_specs=o_spec,
        out_shape=jax.ShapeDtypeStruct((m, n), out_dtype),
        compiler_params=_params(("parallel", "arbitrary")), name=name)(*args)


def _mm_t_body(x_ref, w_ref, o_ref):
    o_ref[...] = lax.dot_general(w_ref[...], x_ref[...], _NT, preferred_element_type=F32).astype(o_ref.dtype)


def _mm_t_rms_body(x_ref, g_ref, w_ref, o_ref):
    xf = x_ref[...].astype(F32)
    y = xf * lax.rsqrt(jnp.mean(xf * xf, -1, keepdims=True) + RMS_EPS) * g_ref[...]
    o_ref[...] = lax.dot_general(w_ref[...], y.astype(BF16), _NT, preferred_element_type=F32).astype(o_ref.dtype)


def matmul_t(x, w_t, *, name, tm, x_col=0, gain=None):
    m = x.shape[0]
    n, k = w_t.shape
    assert m % tm == 0 and x_col % k == 0
    x_spec = pl.BlockSpec((tm, k), lambda i: (i, x_col // k))
    o_spec = pl.BlockSpec((None, n, tm), lambda i: (i, 0, 0))
    if gain is None:
        body, specs, args = _mm_t_body, [x_spec, _resident(w_t.shape)], (x, w_t)
    else:
        g_spec = pl.BlockSpec((1, k), lambda i: (0, 0))
        body, specs, args = _mm_t_rms_body, [x_spec, g_spec, _resident(w_t.shape)], (x, gain.reshape(1, k).astype(F32), w_t)
    return pl.pallas_call(
        body, grid=(m // tm,), in_specs=specs, out_specs=o_spec,
        out_shape=jax.ShapeDtypeStruct((m // tm, n, tm), BF16),
        compiler_params=_params(("parallel",)), name=name)(*args)


def _rope_body(*refs, shift, rms):
    if rms:
        x_ref, c_ref, s1_ref, s2_ref, g_ref, o_ref = refs
    else:
        x_ref, c_ref, s1_ref, s2_ref, o_ref = refs
    x = x_ref[...].astype(F32)
    if rms:
        x = x * lax.rsqrt(jnp.mean(x * x, -1, keepdims=True) + RMS_EPS) * g_ref[...]
    out = x * c_ref[...] + pltpu.roll(x, LANES - shift, 1) * s1_ref[...] + pltpu.roll(x, shift, 1) * s2_ref[...]
    o_ref[...] = out.astype(o_ref.dtype)


def rope(x, tabs, *, name, batch, seq, col, heads, shift, gains=None, ts=1024):
    ts = _tile(seq, ts)
    ns = seq // ts
    blk = col // LANES
    x_spec = pl.BlockSpec((ts, LANES), lambda b, s, h: (b * ns + s, blk + h))
    t_spec = pl.BlockSpec((ts, LANES), lambda b, s, h: (s, 0))
    o_spec = pl.BlockSpec((ts, LANES), lambda b, s, h: (b * ns + s, h))
    specs, args = [x_spec, t_spec, t_spec, t_spec], [x, *tabs]
    if gains is not None:
        specs.append(pl.BlockSpec((None, 1, LANES), lambda b, s, h: (h, 0, 0)))
        args.append(gains.reshape(heads, 1, LANES).astype(F32))
    return pl.pallas_call(
        functools.partial(_rope_body, shift=shift, rms=gains is not None),
        grid=(batch, ns, heads), in_specs=specs, out_specs=o_spec,
        out_shape=jax.ShapeDtypeStruct((batch * seq, heads * LANES), BF16),
        compiler_params=_params(("parallel", "arbitrary", "arbitrary")), name=name)(*args)


def _rope_tables(seq):
    def table(pos, dim, theta):
        inv = theta ** (-jnp.arange(0, dim, 2, dtype=F32) / dim)
        ang = pos.astype(F32)[:, None] * inv[None, :]
        return jnp.cos(ang), jnp.sin(ang)

    pos = jnp.arange(seq)
    row = pos // GRID_W
    colp = pos % GRID_W
    one = lambda n: jnp.ones((seq, n), F32)
    zero = lambda n: jnp.zeros((seq, n), F32)
    cat = lambda *xs: jnp.concatenate(xs, -1)
    ca, sa = table(pos, PARTIAL_ROT, ROPE_THETA)
    ra = LANES - PARTIAL_ROT
    tab_a = (cat(ca, ca, one(ra)), cat(-sa, zero(LANES - 16)), cat(zero(16), sa, zero(ra)))
    cb, sb = table(pos, B_ROPE, ROPE_THETA)
    tab_b = (cat(cb, cb, one(64)), cat(-sb, zero(96)), cat(zero(32), sb, zero(64)))
    cr, sr = table(row, HEAD_DIM // 2, AXIAL_THETA)
    cc, sc = table(colp, HEAD_DIM // 2, AXIAL_THETA)
    tab_c = (cat(cr, cr, cc, cc), cat(-sr, zero(32), -sc, zero(32)), cat(zero(32), sr, zero(32), sc))
    return tab_a, tab_b, tab_c


FLASH_KEYS = 512
LOG2E = 1.4426950408889634


FLASH_STRIP = 512
ONES_ROWS = 16


def _flash_body(*refs, nq, nk, group, tq, nchunks):
    q_refs, k_refs = refs[:nq], refs[nq:nq + nk]
    vt_ref, o_ref, qt_scr, m_scr, mblk_scr, acc_scr, s_even, s_odd = refs[nq + nk:]
    if group > 1:
        for g in range(group):
            qt_scr[:, g * tq:(g + 1) * tq] = q_refs[0][:, g * LANES:(g + 1) * LANES].astype(F32).T.astype(BF16)
    else:
        for i, r in enumerate(q_refs):
            qt_scr[i * LANES:(i + 1) * LANES, :] = r[...].astype(F32).T.astype(BF16)
    nq_all = group * tq
    ones = jnp.ones((ONES_ROWS, FLASH_KEYS), BF16)

    def keys(j):
        off = pl.multiple_of(j * FLASH_KEYS, FLASH_KEYS)
        return jnp.concatenate([r[pl.ds(off, FLASH_KEYS), :] for r in k_refs], axis=1)

    s0 = jnp.dot(keys(0), qt_scr[...], preferred_element_type=F32)
    s_even[...] = s0
    mblk_scr[...] = jnp.max(s0, 0, keepdims=True)
    m_scr[...] = jnp.full(m_scr.shape, -jnp.inf, F32)
    acc_scr[...] = jnp.zeros(acc_scr.shape, F32)

    def block(j, s_cur, s_nxt):
        k_next = keys(jnp.minimum(j + 1, nchunks - 1))
        vt1 = jnp.concatenate([vt_ref[j], ones], axis=0)
        m_prev = m_scr[...]
        m_new = jnp.maximum(m_prev, mblk_scr[...])
        alpha = jnp.exp2(m_prev - m_new)
        m_scr[...] = m_new
        for st in range(nq_all // FLASH_STRIP):
            cs = slice(st * FLASH_STRIP, (st + 1) * FLASH_STRIP)
            s_n = jnp.dot(k_next, qt_scr[:, cs], preferred_element_type=F32)
            s_nxt[:, cs] = s_n
            mblk_scr[:, cs] = jnp.max(s_n, 0, keepdims=True)
            p = jnp.exp2(s_cur[:, cs] - m_new[:, cs]).astype(BF16)
            acc_scr[:, cs] = alpha[:, cs] * acc_scr[:, cs] + jnp.dot(vt1, p, preferred_element_type=F32)

    def pair(i, carry):
        block(2 * i, s_even, s_odd)
        block(2 * i + 1, s_odd, s_even)
        return carry

    lax.fori_loop(0, nchunks // 2, pair, 0)
    acc = acc_scr[...]
    o_t = acc[:LANES] / acc[LANES:LANES + 1]
    for g in range(group):
        o_ref[:, g * LANES:(g + 1) * LANES] = o_t[:, g * tq:(g + 1) * tq].T.astype(o_ref.dtype)


def flash_attention(q_srcs, k_srcs, v_t, *, batch, seq, kv_heads, group, name, rows=1024):
    assert seq % (2 * FLASH_KEYS) == 0
    tq = _tile(seq, rows // group)
    nqb, nchunks = seq // tq, seq // FLASH_KEYS
    nq, nk = len(q_srcs), len(k_srcs)
    qw = group * LANES
    specs, args = [], []
    for arr, col, stride in q_srcs:
        assert col % qw == 0 and stride % qw == 0
        specs.append(pl.BlockSpec((tq, qw), lambda b, h, i, c=col // qw, st=stride // qw: (b * nqb + i, c + h * st)))
        args.append(arr)
    for arr, col, stride in k_srcs:
        specs.append(pl.BlockSpec((seq, LANES), lambda b, h, i, c=col // LANES, st=stride // LANES: (b, c + h * st)))
        args.append(arr)
    specs.append(pl.BlockSpec((nchunks, LANES, FLASH_KEYS), lambda b, h, i: (b, h, 0)))
    args.append(v_t)
    r = group * tq
    assert r % FLASH_STRIP == 0
    dq = LANES if group > 1 else nq * LANES
    return pl.pallas_call(
        functools.partial(_flash_body, nq=nq, nk=nk, group=group, tq=tq, nchunks=nchunks),
        grid=(batch, kv_heads, nqb), in_specs=specs,
        out_specs=pl.BlockSpec((tq, qw), lambda b, h, i: (b * nqb + i, h)),
        out_shape=jax.ShapeDtypeStruct((batch * seq, kv_heads * qw), BF16),
        scratch_shapes=[pltpu.VMEM((dq, r), BF16),
                        pltpu.VMEM((1, r), F32), pltpu.VMEM((1, r), F32),
                        pltpu.VMEM((LANES + ONES_ROWS, r), F32),
                        pltpu.VMEM((FLASH_KEYS, r), F32), pltpu.VMEM((FLASH_KEYS, r), F32)],
        compiler_params=_params(("parallel", "arbitrary", "arbitrary")), name=name)(*args)


A_QB = 256
A_RADIUS = 64


def _band_window(seq, dilation):
    return min(A_QB + 2 * A_RADIUS * dilation, seq)


def _band_body(*refs, seq, dilations, scale):
    ng = len(dilations)
    q_refs, k_ref, v_ref = refs[:ng], refs[ng], refs[ng + 1]
    bias_refs = refs[ng + 2:-1]
    o_ref = refs[-1]
    t0 = pl.program_id(2) * A_QB
    outs, lses = [], []
    nbias = 0
    for g, d in enumerate(dilations):
        w = _band_window(seq, d)
        start = pl.multiple_of(jnp.clip(t0 - A_RADIUS * d, 0, seq - w), A_RADIUS)
        kw = k_ref[pl.ds(start, w), :]
        vw = v_ref[pl.ds(start, w), :]
        s = lax.dot_general(q_refs[g][...], kw, _NT, preferred_element_type=F32) * scale
        if d > 1:
            s = s + bias_refs[nbias][...]
            nbias += 1
        rel = (lax.broadcasted_iota(jnp.int32, (A_QB, w), 1) - lax.broadcasted_iota(jnp.int32, (A_QB, w), 0)
               + (start - t0)).astype(F32)
        s = jnp.where(jnp.abs(rel) <= float(A_RADIUS * d), s, MASK_VALUE)
        m = jnp.max(s, -1, keepdims=True)
        p = jnp.exp(s - m)
        den = jnp.sum(p, -1, keepdims=True)
        outs.append(jnp.dot((p / den).astype(BF16), vw, preferred_element_type=F32))
        lses.append(m + jnp.log(den))
    mx = functools.reduce(jnp.maximum, lses)
    es = [jnp.exp(l - mx) for l in lses]
    tot = functools.reduce(lambda a, b: a + b, es)
    o_ref[...] = functools.reduce(lambda a, b: a + b, [(e / tot) * o for e, o in zip(es, outs)]).astype(o_ref.dtype)


def band_attention(qk, h, *, batch, seq):
    assert seq % A_QB == 0
    dilations = tuple(d for _, d in A_PATTERNS)
    nqb = seq // A_QB
    q_specs = [pl.BlockSpec((A_QB, LANES), lambda b, hh, i, g=g: (b * nqb + i, g * A_KV_HEADS + hh))
               for g in range(len(dilations))]
    k_spec = pl.BlockSpec((seq, LANES), lambda b, hh, i: (b, A_Q_HEADS + hh))
    v_spec = pl.BlockSpec((seq, LANES), lambda b, hh, i: (b, AV_OFF // LANES + hh))
    biases = []
    for d in dilations:
        if d > 1:
            w = _band_window(seq, d)
            off_phase = (np.arange(w)[None, :] - np.arange(A_QB)[:, None]) % d != 0
            biases.append(jnp.asarray(np.where(off_phase, MASK_VALUE, 0.0), F32))
    b_specs = [_resident(b.shape) for b in biases]
    return pl.pallas_call(
        functools.partial(_band_body, seq=seq, dilations=dilations, scale=HEAD_DIM ** -0.5),
        grid=(batch, A_KV_HEADS, nqb), in_specs=q_specs + [k_spec, v_spec] + b_specs,
        out_specs=pl.BlockSpec((A_QB, LANES), lambda b, hh, i: (b * nqb + i, hh)),
        out_shape=jax.ShapeDtypeStruct((batch * seq, A_KV_HEADS * LANES), BF16),
        compiler_params=_params(("parallel", "arbitrary", "arbitrary")), name="band_attention")(
            *([qk] * len(dilations)), qk, h, *biases)


def _sigmoid(x):
    return 1.0 / (1.0 + jnp.exp(-x))


def _layer_norm(y, g, b):
    mu = jnp.mean(y, -1, keepdims=True)
    yc = y - mu
    var = jnp.mean(yc * yc, -1, keepdims=True)
    return yc * lax.rsqrt(var + LN_EPS) * g + b


def _merge_body(oa_ref, ob_ref, oc_ref, g0_ref, g1_ref, g2_ref, wa_ref, wb_ref, wc_ref, o_ref):
    ya = jnp.dot(oa_ref[...], wa_ref[...], preferred_element_type=F32)
    yb = jnp.dot(ob_ref[...], wb_ref[...], preferred_element_type=F32)
    yc = jnp.dot(oc_ref[...], wc_ref[...], preferred_element_type=F32)
    merged = (_sigmoid(g0_ref[...].astype(F32)) * ya + _sigmoid(g1_ref[...].astype(F32)) * yb
              + _sigmoid(g2_ref[...].astype(F32)) * yc)
    o_ref[...] = merged.astype(o_ref.dtype)


def merge_mixers(oa, ob, oc, h, w_o_a, w_o_b, w_o_c, *, tm=256):
    n = ob.shape[0]
    tm = _tile(n, tm)
    row = lambda w: pl.BlockSpec((tm, w), lambda i: (i, 0))
    gate = lambda j: pl.BlockSpec((tm, D_MODEL), lambda i: (i, GATE_OFF // D_MODEL + j))
    specs = ([row(oa.shape[1]), row(ob.shape[1]), row(oc.shape[1]), gate(0), gate(1), gate(2)]
             + [_resident(w_o_a.shape), _resident(w_o_b.shape), _resident(w_o_c.shape)])
    return pl.pallas_call(
        _merge_body, grid=(n // tm,), in_specs=specs, out_specs=row(D_MODEL),
        out_shape=jax.ShapeDtypeStruct((n, D_MODEL), BF16),
        compiler_params=_params(("parallel",)), name="merge_mixers")(oa, ob, oc, h, h, h, w_o_a, w_o_b, w_o_c)


def _proj_ln_body(a_ref, w_ref, x_ref, g_ref, b_ref, o_ref, ob_ref, *, alpha):
    y = alpha * x_ref[...] + jnp.dot(a_ref[...], w_ref[...], preferred_element_type=F32)
    out = _layer_norm(y, g_ref[...], b_ref[...])
    o_ref[...] = out
    ob_ref[...] = out.astype(BF16)


def proj_residual_ln(a, w, x, g, b, *, alpha, tm=256):
    n, d = x.shape
    tm = _tile(n, tm)
    row = lambda wd: pl.BlockSpec((tm, wd), lambda i: (i, 0))
    vec = pl.BlockSpec((1, d), lambda i: (0, 0))
    return pl.pallas_call(
        functools.partial(_proj_ln_body, alpha=alpha), grid=(n // tm,),
        in_specs=[row(a.shape[1]), _resident(w.shape), row(d), vec, vec],
        out_specs=[row(d), row(d)],
        out_shape=[jax.ShapeDtypeStruct((n, d), F32), jax.ShapeDtypeStruct((n, d), BF16)],
        compiler_params=_params(("parallel",)), name="proj_residual_ln")(a, w, x, g.reshape(1, d), b.reshape(1, d))


def _split_bf16(x):
    hi = x.astype(BF16)
    lo = (x - hi.astype(F32)).astype(BF16)
    return hi, lo


def _router_body(x_ref, wh_ref, wl_ref, bias_ref, idx_ref, wt_ref):
    xh, xl = _split_bf16(x_ref[...])
    nt = (((1,), (1,)), ((), ()))
    wh, wl = wh_ref[...], wl_ref[...]
    logits = (lax.dot_general(wh, xh, nt, preferred_element_type=F32)
              + (lax.dot_general(wh, xl, nt, preferred_element_type=F32)
                 + lax.dot_general(wl, xh, nt, preferred_element_type=F32)))
    scores = _sigmoid(logits)
    biased = scores + bias_ref[...]
    tn = scores.shape[1]
    gsz = N_EXPERTS // N_EXPERT_GROUPS
    neg = -jnp.inf
    sub = lax.broadcasted_iota(jnp.int32, (gsz, tn), 0).astype(F32)
    gs = []
    for g in range(N_EXPERT_GROUPS):
        blk = biased[g * gsz:(g + 1) * gsz]
        m1 = jnp.max(blk, 0, keepdims=True)
        first = jnp.min(jnp.where(blk == m1, sub, float(gsz)), 0, keepdims=True)
        m2 = jnp.max(jnp.where(sub == first, neg, blk), 0, keepdims=True)
        gs.append(m1 + m2)
    gs = jnp.concatenate(gs, axis=0)
    gi = lax.broadcasted_iota(jnp.int32, (N_EXPERT_GROUPS, tn), 0).astype(F32)
    gsel = jnp.zeros((N_EXPERT_GROUPS, tn), F32)
    for _ in range(TOPK_GROUPS):
        gm = jnp.max(gs, 0, keepdims=True)
        pick = gi == jnp.min(jnp.where(gs == gm, gi, float(N_EXPERT_GROUPS)), 0, keepdims=True)
        gsel = jnp.where(pick, 1.0, gsel)
        gs = jnp.where(pick, neg, gs)
    masked = jnp.concatenate(
        [jnp.where(gsel[g:g + 1] > 0.0, biased[g * gsz:(g + 1) * gsz], neg) for g in range(N_EXPERT_GROUPS)], axis=0)
    ei = lax.broadcasted_iota(jnp.int32, (N_EXPERTS, tn), 0).astype(F32)
    idxs, wts = [], []
    for _ in range(TOP_K):
        mx = jnp.max(masked, 0, keepdims=True)
        ix = jnp.min(jnp.where(masked == mx, ei, float(N_EXPERTS)), 0, keepdims=True)
        pick = ei == ix
        idxs.append(ix)
        wts.append(jnp.sum(jnp.where(pick, scores, 0.0), 0, keepdims=True))
        masked = jnp.where(pick, neg, masked)
    w = jnp.concatenate(wts, axis=0)
    idx_ref[...] = jnp.concatenate(idxs, axis=0).astype(jnp.int32)
    wt_ref[...] = w / jnp.sum(w, 0, keepdims=True) * ROUTED_SCALE


def router(x, w_router, router_bias, *, tn=512):
    n, d = x.shape
    tn = _tile(n, tn)
    wt = w_router.T.astype(F32)
    wh = wt.astype(BF16)
    wl = (wt - wh.astype(F32)).astype(BF16)
    full = lambda r, c: pl.BlockSpec((r, c), lambda i: (0, 0))
    out = pl.BlockSpec((TOP_K, tn), lambda i: (0, i))
    return pl.pallas_call(
        _router_body, grid=(n // tn,),
        in_specs=[pl.BlockSpec((tn, d), lambda i: (i, 0)), full(N_EXPERTS, d), full(N_EXPERTS, d), full(N_EXPERTS, 1)],
        out_specs=[out, out],
        out_shape=[jax.ShapeDtypeStruct((TOP_K, n), jnp.int32), jax.ShapeDtypeStruct((TOP_K, n), F32)],
        compiler_params=_params(("parallel",)), name="router")(x, wh, wl, router_bias.reshape(N_EXPERTS, 1).astype(F32))


def _silu(x):
    return x * _sigmoid(x)


def _experts_body(blk_e_ref, nused_ref, x_ref, wgu_ref, wd_ref, o_ref):
    i = pl.program_id(0)

    @pl.when(i < nused_ref[0])
    def _():
        hgu = jnp.dot(x_ref[...], wgu_ref[...], preferred_element_type=F32)
        a = _silu(hgu[:, :EXPERT_FF]) * hgu[:, EXPERT_FF:]
        o_ref[...] = jnp.dot(a.astype(BF16), wd_ref[...], preferred_element_type=F32)

    @pl.when(i >= nused_ref[0])
    def _():
        o_ref[...] = jnp.zeros(o_ref.shape, o_ref.dtype)


def grouped_experts(xs, blk_e, nused, w_gu, w_d):
    p, d = xs.shape
    nb = p // MOE_ROWS
    grid_spec = pltpu.PrefetchScalarGridSpec(
        num_scalar_prefetch=2, grid=(nb,),
        in_specs=[pl.BlockSpec((MOE_ROWS, d), lambda i, e, u: (i, 0)),
                  pl.BlockSpec((None, d, 2 * EXPERT_FF), lambda i, e, u: (e[i], 0, 0)),
                  pl.BlockSpec((None, EXPERT_FF, d), lambda i, e, u: (e[i], 0, 0))],
        out_specs=pl.BlockSpec((MOE_ROWS, d), lambda i, e, u: (i, 0)))
    return pl.pallas_call(
        _experts_body, grid_spec=grid_spec, out_shape=jax.ShapeDtypeStruct((p, d), F32),
        compiler_params=_params(("arbitrary",)), name="grouped_experts")(blk_e, nused, xs, w_gu, w_d)


def _shared_ln_body(x_ref, xb_ref, r_ref, wgu_ref, wd_ref, g_ref, b_ref, o_ref, ob_ref, *, alpha):
    hgu = jnp.dot(xb_ref[...], wgu_ref[...], preferred_element_type=F32)
    a = _silu(hgu[:, :SHARED_FF]) * hgu[:, SHARED_FF:]
    y = jnp.dot(a.astype(BF16), wd_ref[...], preferred_element_type=F32) + r_ref[...]
    out = _layer_norm(alpha * x_ref[...] + y, g_ref[...], b_ref[...])
    o_ref[...] = out
    ob_ref[...] = out.astype(BF16)


def shared_expert_ln(x, xb, routed, w_gu, w_d, g, b, *, alpha, tm=256):
    n, d = x.shape
    tm = _tile(n, tm)
    row = pl.BlockSpec((tm, d), lambda i: (i, 0))
    vec = pl.BlockSpec((1, d), lambda i: (0, 0))
    return pl.pallas_call(
        functools.partial(_shared_ln_body, alpha=alpha), grid=(n // tm,),
        in_specs=[row, row, row, _resident(w_gu.shape), _resident(w_d.shape), vec, vec],
        out_specs=[row, row],
        out_shape=[jax.ShapeDtypeStruct((n, d), F32), jax.ShapeDtypeStruct((n, d), BF16)],
        compiler_params=_params(("parallel",)), name="shared_expert_ln")(x, xb, routed, w_gu, w_d, g.reshape(1, d), b.reshape(1, d))


def _dispatch_plan(idx, n):
    a = n * TOP_K
    flat_e = idx.T.reshape(a)
    iota = jnp.arange(a, dtype=jnp.int32)
    _, order = lax.sort_key_val(flat_e, iota)
    _, rank = lax.sort_key_val(order, iota)
    experts = jnp.arange(N_EXPERTS, dtype=jnp.int32)
    counts = jnp.sum((flat_e[:, None] == experts[None, :]).astype(jnp.int32), axis=0)
    padded = (counts + MOE_ROWS - 1) // MOE_ROWS * MOE_ROWS
    pad_end = jnp.cumsum(padded)
    pad_start = pad_end - padded
    start = jnp.cumsum(counts) - counts
    n_blocks = a // MOE_ROWS + N_EXPERTS
    blk0 = jnp.arange(n_blocks, dtype=jnp.int32) * MOE_ROWS
    blk_e = jnp.minimum(jnp.sum((pad_end[None, :] <= blk0[:, None]).astype(jnp.int32), axis=1), N_EXPERTS - 1)
    src = (start - pad_start)[blk_e][:, None] + blk0[:, None] + jnp.arange(MOE_ROWS, dtype=jnp.int32)[None, :]
    slot_token = order[jnp.clip(src, 0, a - 1).reshape(-1)] // TOP_K
    nused = (pad_end[-1] // MOE_ROWS).astype(jnp.int32).reshape(1)
    slot_of = ((pad_start - start)[flat_e] + rank).reshape(n, TOP_K)
    return slot_token, blk_e.astype(jnp.int32), nused, slot_of


def _prep_layer(l, p):
    bf = lambda a: a.astype(BF16)
    w_in = p['w_in'][l]
    offs = np.cumsum((0, A_Q_HEADS * HEAD_DIM, A_KV_HEADS * HEAD_DIM, A_KV_HEADS * HEAD_DIM, B_Q_RANK, B_KV_RANK,
                      B_ROPE, C_Q_HEADS * HEAD_DIM, C_KV_HEADS * HEAD_DIM, C_KV_HEADS * HEAD_DIM, N_BRANCH * D_MODEL))
    piece = lambda i: w_in[:, offs[i]:offs[i + 1]]
    zeros = lambda c: jnp.zeros((D_MODEL, c), w_in.dtype)
    w_h = jnp.concatenate([piece(9), piece(0), piece(1), piece(2), piece(3), piece(4), piece(5), zeros(LANES - B_ROPE),
                           piece(6), piece(7), piece(8), zeros(H_COLS - H_USED)], axis=1)
    w_uq = p['w_uq'][l].reshape(B_Q_RANK, B_HEADS, B_NOPE + B_ROPE)
    w_uq_rope = jnp.pad(w_uq[:, :, B_NOPE:], ((0, 0), (0, 0), (0, LANES - B_ROPE)))
    w_q = jnp.concatenate([w_uq[:, :, :B_NOPE].reshape(B_Q_RANK, -1), w_uq_rope.reshape(B_Q_RANK, -1)], axis=1)
    w_ukv = p['w_ukv'][l].reshape(B_KV_RANK, B_HEADS, B_NOPE + B_V)
    w_k = w_ukv[:, :, :B_NOPE].reshape(B_KV_RANK, -1)
    w_v_t = w_ukv[:, :, B_NOPE:].reshape(B_KV_RANK, -1).T
    return dict(
        w_h=bf(w_h), w_q=bf(w_q), w_k=bf(w_k), w_v_t=bf(w_v_t), w_cv_t=bf(piece(8).T),
        w_o_a=bf(p['w_o_a'][l]), w_o_b=bf(p['w_o_b'][l]), w_o_c=bf(p['w_o_c'][l]), w_out=bf(p['w_out'][l]),
        w_e_gu=bf(jnp.concatenate([p['w_e_gate'][l], p['w_e_up'][l]], axis=-1)), w_e_d=bf(p['w_e_down'][l]),
        w_s_gu=bf(jnp.concatenate([p['w_s_gate'][l], p['w_s_up'][l]], axis=-1)), w_s_d=bf(p['w_s_down'][l]),
        c_gains=jnp.concatenate([jnp.tile(p['c_q_norm'][l][None] * (HEAD_DIM ** -0.5 * LOG2E), (C_Q_HEADS, 1)),
                                 jnp.tile(p['c_k_norm'][l][None], (C_KV_HEADS, 1))], axis=0),
    )


def _token_mixer(x, xb, lw, l, p, tabs, batch, seq, alpha):
    tab_a, tab_b, tab_c = tabs
    h = matmul(xb, lw['w_h'], name="in_proj")
    assert all((window // dilation) // 2 == A_RADIUS for window, dilation in A_PATTERNS)
    qk_a = rope(h, tab_a, name="rope_a", batch=batch, seq=seq, col=AQ_OFF, heads=A_Q_HEADS + A_KV_HEADS,
                shift=PARTIAL_ROT // 2)
    o_a = band_attention(qk_a, h, batch=batch, seq=seq)
    q_b = matmul(h, lw['w_q'], name="mla_q", x_col=BCQ_OFF, gain=p['mla_q_norm'][l],
                 out_scale=(B_NOPE + B_ROPE) ** -0.5 * LOG2E)
    k_b = matmul(h, lw['w_k'], name="mla_k", x_col=BCKV_OFF, gain=p['mla_kv_norm'][l])
    vt_b = matmul_t(h, lw['w_v_t'], name="mla_v_t", tm=FLASH_KEYS, x_col=BCKV_OFF, gain=p['mla_kv_norm'][l])
    q_b_rope = rope(q_b, tab_b, name="rope_bq", batch=batch, seq=seq, col=B_HEADS * LANES, heads=B_HEADS,
                    shift=B_ROPE // 2)
    k_b_rope = rope(h, tab_b, name="rope_bk", batch=batch, seq=seq, col=BKR_OFF, heads=1, shift=B_ROPE // 2)
    o_b = flash_attention([(q_b, 0, LANES), (q_b_rope, 0, LANES)], [(k_b, 0, LANES), (k_b_rope, 0, 0)], vt_b,
                          batch=batch, seq=seq, kv_heads=B_HEADS, group=1, name="flash_b")
    qk_c = rope(h, tab_c, name="rope_c", batch=batch, seq=seq, col=CQ_OFF, heads=C_Q_HEADS + C_KV_HEADS,
                shift=HEAD_DIM // 4, gains=lw['c_gains'])
    vt_c = matmul_t(xb, lw['w_cv_t'], name="c_v_t", tm=FLASH_KEYS)
    group = C_Q_HEADS // C_KV_HEADS
    o_c = flash_attention([(qk_c, 0, group * LANES)], [(qk_c, C_Q_HEADS * LANES, LANES)], vt_c,
                          batch=batch, seq=seq, kv_heads=C_KV_HEADS, group=group, name="flash_c")
    merged = merge_mixers(o_a, o_b, o_c, h, lw['w_o_a'], lw['w_o_b'], lw['w_o_c'])
    return proj_residual_ln(merged, lw['w_out'], x, p['ln1_g'][l], p['ln1_b'][l], alpha=alpha)


def _moe(x, xb, lw, l, p, alpha):
    n = x.shape[0]
    idx, wts = router(x, p['w_router'][l], p['router_bias'][l])
    slot_token, blk_e, nused, slot_of = _dispatch_plan(idx, n)
    xs = xb.at[slot_token].get(mode='promise_in_bounds')
    y = grouped_experts(xs, blk_e, nused, lw['w_e_gu'], lw['w_e_d'])
    routed = jnp.sum(y.at[slot_of].get(mode='promise_in_bounds') * wts.T[:, :, None], axis=1)
    return shared_expert_ln(x, xb, routed, lw['w_s_gu'], lw['w_s_d'], p['ln2_g'][l], p['ln2_b'][l], alpha=alpha)


def _run_trunk(x3, layers, p):
    batch, seq, d = x3.shape
    depth = len(layers)
    alpha = (2 * depth) ** 0.25
    tabs = _rope_tables(seq)
    x = x3.reshape(batch * seq, d)
    xb = x.astype(BF16)
    for l, lw in enumerate(layers):
        x, xb = _token_mixer(x, xb, lw, l, p, tabs, batch, seq, alpha)
        x, xb = _moe(x, xb, lw, l, p, alpha)
    return x.reshape(batch, seq, d)


def kernel(x_prompt, x_sample, ln1_g, ln1_b, w_in, mla_q_norm, w_uq, mla_kv_norm, w_ukv, c_q_norm, c_k_norm, w_o_a, w_o_b, w_o_c, w_out, ln2_g, ln2_b, w_router, router_bias, w_e_gate, w_e_up, w_e_down, w_s_gate, w_s_up, w_s_down):
    p = dict(ln1_g=ln1_g, ln1_b=ln1_b, w_in=w_in, mla_q_norm=mla_q_norm, w_uq=w_uq, mla_kv_norm=mla_kv_norm,
             w_ukv=w_ukv, c_q_norm=c_q_norm, c_k_norm=c_k_norm, w_o_a=w_o_a, w_o_b=w_o_b, w_o_c=w_o_c, w_out=w_out,
             ln2_g=ln2_g, ln2_b=ln2_b, w_router=w_router, router_bias=router_bias, w_e_gate=w_e_gate,
             w_e_up=w_e_up, w_e_down=w_e_down, w_s_gate=w_s_gate, w_s_up=w_s_up, w_s_down=w_s_down)
    layers = [_prep_layer(l, p) for l in range(w_in.shape[0])]
    return _run_trunk(x_prompt, layers, p), _run_trunk(x_sample, layers, p)
```

```python
import functools

import numpy as np
import jax
import jax.numpy as jnp
from jax import lax
from jax.experimental import pallas as pl
from jax.experimental.pallas import tpu as pltpu

F32 = jnp.float32
BF16 = jnp.bfloat16

D_MODEL = 2048
HEAD_DIM = 128
A_PATTERNS = ((128, 1), (512, 4), (2048, 16))
A_KV_HEADS = 4
A_Q_HEADS = len(A_PATTERNS) * A_KV_HEADS
ROPE_THETA = 500000.0
PARTIAL_ROT = HEAD_DIM // 4
B_HEADS = 8
B_Q_RANK = 512
B_KV_RANK = 256
B_NOPE = 128
B_ROPE = 64
B_V = 128
C_Q_HEADS = 8
C_KV_HEADS = 2
AXIAL_THETA = 10000.0
GRID_W = 64
N_BRANCH = 3
N_EXPERTS = 64
TOP_K = 8
N_EXPERT_GROUPS = 8
TOPK_GROUPS = 4
EXPERT_FF = 512
SHARED_FF = 512
ROUTED_SCALE = 2.5
LN_EPS = 1e-5
RMS_EPS = 1e-6
MASK_VALUE = -1e30

LANES = 128
VMEM_LIMIT = 56 * 1024 * 1024
_NT = (((1,), (1,)), ((), ()))

GATE_OFF = 0
AQ_OFF = GATE_OFF + N_BRANCH * D_MODEL
AK_OFF = AQ_OFF + A_Q_HEADS * HEAD_DIM
AV_OFF = AK_OFF + A_KV_HEADS * HEAD_DIM
BCQ_OFF = AV_OFF + A_KV_HEADS * HEAD_DIM
BCKV_OFF = BCQ_OFF + B_Q_RANK
BKR_OFF = BCKV_OFF + B_KV_RANK
CQ_OFF = BKR_OFF + LANES
CK_OFF = CQ_OFF + C_Q_HEADS * HEAD_DIM
CV_OFF = CK_OFF + C_KV_HEADS * HEAD_DIM
H_USED = CV_OFF + C_KV_HEADS * HEAD_DIM
H_COLS = -(-H_USED // 1024) * 1024

MOE_ROWS = 256


def _tile(n, t):
    t = min(n, t)
    assert n % t == 0, (n, t)
    return t


def _params(sem):
    return pltpu.CompilerParams(dimension_semantics=sem, vmem_limit_bytes=VMEM_LIMIT)


def _resident(shape):
    nd = len(shape)
    return pl.BlockSpec(shape, lambda *_: (0,) * nd, pipeline_mode=pl.Buffered(1))


def _mm_body(x_ref, w_ref, o_ref):
    o_ref[...] = jnp.dot(x_ref[...], w_ref[...], preferred_element_type=F32).astype(o_ref.dtype)


def _mm_rms_body(x_ref, g_ref, w_ref, o_ref, *, out_scale):
    xf = x_ref[...].astype(F32)
    y = xf * lax.rsqrt(jnp.mean(xf * xf, -1, keepdims=True) + RMS_EPS) * g_ref[...]
    acc = jnp.dot(y.astype(BF16), w_ref[...], preferred_element_type=F32)
    o_ref[...] = (acc * out_scale).astype(o_ref.dtype)


def matmul(x, w, *, name, x_col=0, gain=None, out_scale=1.0, tm=1024, tn=1024, out_dtype=BF16):
    m = x.shape[0]
    k, n = w.shape
    tm, tn = _tile(m, tm), _tile(n, tn)
    assert x_col % k == 0
    x_spec = pl.BlockSpec((tm, k), lambda i, j: (i, x_col // k))
    w_spec = pl.BlockSpec((k, tn), lambda i, j: (0, j))
    o_spec = pl.BlockSpec((tm, tn), lambda i, j: (i, j))
    if gain is None:
        assert out_scale == 1.0
        body, specs, args = _mm_body, [x_spec, w_spec], (x, w)
    else:
        g_spec = pl.BlockSpec((1, k), lambda i, j: (0, 0))
        body = functools.partial(_mm_rms_body, out_scale=out_scale)
        specs, args = [x_spec, g_spec, w_spec], (x, gain.reshape(1, k).astype(F32), w)
    return pl.pallas_call(
        body, grid=(m // tm, n // tn), in_specs=specs, out_specs=o_spec,
        out_shape=jax.ShapeDtypeStruct((m, n), out_dtype),
        compiler_params=_params(("parallel", "arbitrary")), name=name)(*args)


def _mm_t_body(x_ref, w_ref, o_ref):
    o_ref[...] = lax.dot_general(w_ref[...], x_ref[...], _NT, preferred_element_type=F32).astype(o_ref.dtype)


def _mm_t_rms_body(x_ref, g_ref, w_ref, o_ref):
    xf = x_ref[...].astype(F32)
    y = xf * lax.rsqrt(jnp.mean(xf * xf, -1, keepdims=True) + RMS_EPS) * g_ref[...]
    o_ref[...] = lax.dot_general(w_ref[...], y.astype(BF16), _NT, preferred_element_type=F32).astype(o_ref.dtype)


def matmul_t(x, w_t, *, name, tm, x_col=0, gain=None):
    m = x.shape[0]
    n, k = w_t.shape
    assert m % tm == 0 and x_col % k == 0
    x_spec = pl.BlockSpec((tm, k), lambda i: (i, x_col // k))
    o_spec = pl.BlockSpec((None, n, tm), lambda i: (i, 0, 0))
    if gain is None:
        body, specs, args = _mm_t_body, [x_spec, _resident(w_t.shape)], (x, w_t)
    else:
        g_spec = pl.BlockSpec((1, k), lambda i: (0, 0))
        body, specs, args = _mm_t_rms_body, [x_spec, g_spec, _resident(w_t.shape)], (x, gain.reshape(1, k).astype(F32), w_t)
    return pl.pallas_call(
        body, grid=(m // tm,), in_specs=specs, out_specs=o_spec,
        out_shape=jax.ShapeDtypeStruct((m // tm, n, tm), BF16),
        compiler_params=_params(("parallel",)), name=name)(*args)


def _rope_body(*refs, shift, rms):
    if rms:
        x_ref, c_ref, s1_ref, s2_ref, g_ref, o_ref = refs
    else:
        x_ref, c_ref, s1_ref, s2_ref, o_ref = refs
    x = x_ref[...].astype(F32)
    if rms:
        x = x * lax.rsqrt(jnp.mean(x * x, -1, keepdims=True) + RMS_EPS) * g_ref[...]
    out = x * c_ref[...] + pltpu.roll(x, LANES - shift, 1) * s1_ref[...] + pltpu.roll(x, shift, 1) * s2_ref[...]
    o_ref[...] = out.astype(o_ref.dtype)


def rope(x, tabs, *, name, segs, col, heads, shift, gains=None, ts=1024):
    n = x.shape[0]
    ts = min([ts] + [seq for _, _, seq in segs])
    assert n % ts == 0 and all(row0 % ts == 0 and seq % ts == 0 for row0, _, seq in segs)
    blk = col // LANES

    def tab_block(r):
        idx = r * 0
        for row0, _, seq in segs:
            idx = jnp.where(r >= row0 // ts, (r - row0 // ts) % (seq // ts), idx)
        return idx

    x_spec = pl.BlockSpec((ts, LANES), lambda r, h: (r, blk + h))
    t_spec = pl.BlockSpec((ts, LANES), lambda r, h: (tab_block(r), 0))
    o_spec = pl.BlockSpec((ts, LANES), lambda r, h: (r, h))
    specs, args = [x_spec, t_spec, t_spec, t_spec], [x, *tabs]
    if gains is not None:
        specs.append(pl.BlockSpec((None, 1, LANES), lambda r, h: (h, 0, 0)))
        args.append(gains.reshape(heads, 1, LANES).astype(F32))
    return pl.pallas_call(
        functools.partial(_rope_body, shift=shift, rms=gains is not None),
        grid=(n // ts, heads), in_specs=specs, out_specs=o_spec,
        out_shape=jax.ShapeDtypeStruct((n, heads * LANES), BF16),
        compiler_params=_params(("parallel", "arbitrary")), name=name)(*args)


def _rope_tables(seq):
    def table(pos, dim, theta):
        inv = theta ** (-jnp.arange(0, dim, 2, dtype=F32) / dim)
        ang = pos.astype(F32)[:, None] * inv[None, :]
        return jnp.cos(ang), jnp.sin(ang)

    pos = jnp.arange(seq)
    row = pos // GRID_W
    colp = pos % GRID_W
    one = lambda n: jnp.ones((seq, n), F32)
    zero = lambda n: jnp.zeros((seq, n), F32)
    cat = lambda *xs: jnp.concatenate(xs, -1)
    ca, sa = table(pos, PARTIAL_ROT, ROPE_THETA)
    ra = LANES - PARTIAL_ROT
    tab_a = (cat(ca, ca, one(ra)), cat(-sa, zero(LANES - 16)), cat(zero(16), sa, zero(ra)))
    cb, sb = table(pos, B_ROPE, ROPE_THETA)
    tab_b = (cat(cb, cb, one(64)), cat(-sb, zero(96)), cat(zero(32), sb, zero(64)))
    cr, sr = table(row, HEAD_DIM // 2, AXIAL_THETA)
    cc, sc = table(colp, HEAD_DIM // 2, AXIAL_THETA)
    tab_c = (cat(cr, cr, cc, cc), cat(-sr, zero(32), -sc, zero(32)), cat(zero(32), sr, zero(32), sc))
    return tab_a, tab_b, tab_c


FLASH_KEYS = 512
LOG2E = 1.4426950408889634


FLASH_STRIP = 512
ONES_ROWS = 16


def _flash_body(*refs, nq, nk, group, tq, nchunks, aliased):
    q_refs, k_refs, vt_ref = refs[:nq], refs[nq:nq + nk], refs[nq + nk]
    o_ref, qt_scr, m_scr, mblk_scr, acc_scr, s_even, s_odd = refs[nq + nk + 1 + aliased:]
    if group > 1:
        for g in range(group):
            qt_scr[:, g * tq:(g + 1) * tq] = q_refs[0][:, g * LANES:(g + 1) * LANES].astype(F32).T.astype(BF16)
    else:
        for i, r in enumerate(q_refs):
            qt_scr[i * LANES:(i + 1) * LANES, :] = r[...].astype(F32).T.astype(BF16)
    nq_all = group * tq
    ones = jnp.ones((ONES_ROWS, FLASH_KEYS), BF16)

    def keys(j):
        off = pl.multiple_of(j * FLASH_KEYS, FLASH_KEYS)
        return jnp.concatenate([r[pl.ds(off, FLASH_KEYS), :] for r in k_refs], axis=1)

    s0 = jnp.dot(keys(0), qt_scr[...], preferred_element_type=F32)
    s_even[...] = s0
    mblk_scr[...] = jnp.max(s0, 0, keepdims=True)
    m_scr[...] = jnp.full(m_scr.shape, -jnp.inf, F32)
    acc_scr[...] = jnp.zeros(acc_scr.shape, F32)

    def block(j, s_cur, s_nxt):
        k_next = keys(jnp.minimum(j + 1, nchunks - 1))
        vt1 = jnp.concatenate([vt_ref[j], ones], axis=0)
        m_prev = m_scr[...]
        m_new = jnp.maximum(m_prev, mblk_scr[...])
        alpha = jnp.exp2(m_prev - m_new)
        m_scr[...] = m_new
        for st in range(nq_all // FLASH_STRIP):
            cs = slice(st * FLASH_STRIP, (st + 1) * FLASH_STRIP)
            s_n = jnp.dot(k_next, qt_scr[:, cs], preferred_element_type=F32)
            s_nxt[:, cs] = s_n
            mblk_scr[:, cs] = jnp.max(s_n, 0, keepdims=True)
            p = jnp.exp2(s_cur[:, cs] - m_new[:, cs]).astype(BF16)
            acc_scr[:, cs] = alpha[:, cs] * acc_scr[:, cs] + jnp.dot(vt1, p, preferred_element_type=F32)

    def pair(i, carry):
        block(2 * i, s_even, s_odd)
        block(2 * i + 1, s_odd, s_even)
        return carry

    lax.fori_loop(0, nchunks // 2, pair, 0)
    acc = acc_scr[...]
    o_t = acc[:LANES] / acc[LANES:LANES + 1]
    for g in range(group):
        o_ref[:, g * LANES:(g + 1) * LANES] = o_t[:, g * tq:(g + 1) * tq].T.astype(o_ref.dtype)


def _alias_previous(specs, args, prev):
    if prev is None:
        return {}
    specs.append(pl.BlockSpec(memory_space=pl.ANY))
    args.append(prev)
    return {len(args) - 1: 0}


def flash_attention(q_srcs, k_srcs, v_t, *, segs, kv_heads, group, name, rows=1024):
    n = q_srcs[0][0].shape[0]
    nq, nk = len(q_srcs), len(k_srcs)
    qw = group * LANES
    dq = LANES if group > 1 else nq * LANES
    out = None
    for row0, batch, seq in segs:
        assert seq % (2 * FLASH_KEYS) == 0 and row0 % seq == 0
        tq = _tile(seq, rows // group)
        nqb, nchunks = seq // tq, seq // FLASH_KEYS
        r = group * tq
        assert r % FLASH_STRIP == 0
        qb0, sb0 = row0 // tq, row0 // seq
        specs, args = [], []
        for arr, col, stride in q_srcs:
            assert col % qw == 0 and stride % qw == 0
            specs.append(pl.BlockSpec((tq, qw), lambda b, h, i, c=col // qw, st=stride // qw, qb0=qb0, nqb=nqb:
                                      (qb0 + b * nqb + i, c + h * st)))
            args.append(arr)
        for arr, col, stride in k_srcs:
            specs.append(pl.BlockSpec((seq, LANES), lambda b, h, i, c=col // LANES, st=stride // LANES, sb0=sb0:
                                      (sb0 + b, c + h * st)))
            args.append(arr)
        specs.append(pl.BlockSpec((nchunks, LANES, FLASH_KEYS), lambda b, h, i, sb0=sb0: (sb0 + b, h, 0)))
        args.append(v_t)
        aliases = _alias_previous(specs, args, out)
        out = pl.pallas_call(
            functools.partial(_flash_body, nq=nq, nk=nk, group=group, tq=tq, nchunks=nchunks, aliased=out is not None),
            grid=(batch, kv_heads, nqb), in_specs=specs,
            out_specs=pl.BlockSpec((tq, qw), lambda b, h, i, qb0=qb0, nqb=nqb: (qb0 + b * nqb + i, h)),
            out_shape=jax.ShapeDtypeStruct((n, kv_heads * qw), BF16),
            scratch_shapes=[pltpu.VMEM((dq, r), BF16),
                            pltpu.VMEM((1, r), F32), pltpu.VMEM((1, r), F32),
                            pltpu.VMEM((LANES + ONES_ROWS, r), F32),
                            pltpu.VMEM((FLASH_KEYS, r), F32), pltpu.VMEM((FLASH_KEYS, r), F32)],
            input_output_aliases=aliases,
            compiler_params=_params(("parallel", "arbitrary", "arbitrary")), name=name)(*args)
    return out


A_QB = 256
A_RADIUS = 64


def _band_window(seq, dilation):
    return min(A_QB + 2 * A_RADIUS * dilation, seq)


OFF_PHASE = 1e9


def _band_body(*refs, seq, dilations, scale, aliased):
    ng = len(dilations)
    q_refs, k_ref, v_ref = refs[:ng], refs[ng], refs[ng + 1]
    rel_refs = refs[ng + 2:2 * ng + 2]
    o_ref = refs[2 * ng + 2 + aliased]
    t0 = pl.program_id(2) * A_QB
    outs, lses = [], []
    for g, d in enumerate(dilations):
        w = _band_window(seq, d)
        start = pl.multiple_of(jnp.clip(t0 - A_RADIUS * d, 0, seq - w), A_RADIUS)
        kw = k_ref[pl.ds(start, w), :]
        vw = v_ref[pl.ds(start, w), :]
        s = lax.dot_general(q_refs[g][...], kw, _NT, preferred_element_type=F32) * scale
        rel = rel_refs[g][...] + (start - t0).astype(F32)
        s = jnp.where(jnp.abs(rel) <= float(A_RADIUS * d), s, MASK_VALUE)
        m = jnp.max(s, -1, keepdims=True)
        p = jnp.exp(s - m)
        den = jnp.sum(p, -1, keepdims=True)
        outs.append(jnp.dot((p / den).astype(BF16), vw, preferred_element_type=F32))
        lses.append(m + jnp.log(den))
    mx = functools.reduce(jnp.maximum, lses)
    es = [jnp.exp(l - mx) for l in lses]
    tot = functools.reduce(lambda a, b: a + b, es)
    o_ref[...] = functools.reduce(lambda a, b: a + b, [(e / tot) * o for e, o in zip(es, outs)]).astype(o_ref.dtype)


def band_attention(qk, h, *, segs):
    n = qk.shape[0]
    dilations = tuple(d for _, d in A_PATTERNS)
    out = None
    for row0, batch, seq in segs:
        assert seq % A_QB == 0 and row0 % seq == 0
        nqb = seq // A_QB
        qb0, sb0 = row0 // A_QB, row0 // seq
        specs = [pl.BlockSpec((A_QB, LANES),
                              lambda b, hh, i, g=g, qb0=qb0, nqb=nqb: (qb0 + b * nqb + i, g * A_KV_HEADS + hh))
                 for g in range(len(dilations))]
        specs.append(pl.BlockSpec((seq, LANES), lambda b, hh, i, sb0=sb0: (sb0 + b, A_Q_HEADS + hh)))
        specs.append(pl.BlockSpec((seq, LANES), lambda b, hh, i, sb0=sb0: (sb0 + b, AV_OFF // LANES + hh)))
        args = [qk] * len(dilations) + [qk, h]
        for d in dilations:
            ji = np.arange(_band_window(seq, d))[None, :] - np.arange(A_QB)[:, None]
            rel = jnp.asarray(np.where(ji % d == 0, ji, OFF_PHASE), F32)
            specs.append(_resident(rel.shape))
            args.append(rel)
        aliases = _alias_previous(specs, args, out)
        out = pl.pallas_call(
            functools.partial(_band_body, seq=seq, dilations=dilations, scale=HEAD_DIM ** -0.5,
                              aliased=out is not None),
            grid=(batch, A_KV_HEADS, nqb), in_specs=specs,
            out_specs=pl.BlockSpec((A_QB, LANES), lambda b, hh, i, qb0=qb0, nqb=nqb: (qb0 + b * nqb + i, hh)),
            out_shape=jax.ShapeDtypeStruct((n, A_KV_HEADS * LANES), BF16),
            input_output_aliases=aliases,
            compiler_params=_params(("parallel", "arbitrary", "arbitrary")), name="band_attention")(*args)
    return out


def _sigmoid(x):
    return 1.0 / (1.0 + jnp.exp(-x))


def _layer_norm(y, g, b):
    mu = jnp.mean(y, -1, keepdims=True)
    yc = y - mu
    var = jnp.mean(yc * yc, -1, keepdims=True)
    return yc * lax.rsqrt(var + LN_EPS) * g + b


def _merge_body(oa_ref, ob_ref, oc_ref, g0_ref, g1_ref, g2_ref, wa_ref, wb_ref, wc_ref, o_ref):
    ya = jnp.dot(oa_ref[...], wa_ref[...], preferred_element_type=F32)
    yb = jnp.dot(ob_ref[...], wb_ref[...], preferred_element_type=F32)
    yc = jnp.dot(oc_ref[...], wc_ref[...], preferred_element_type=F32)
    merged = (_sigmoid(g0_ref[...].astype(F32)) * ya + _sigmoid(g1_ref[...].astype(F32)) * yb
              + _sigmoid(g2_ref[...].astype(F32)) * yc)
    o_ref[...] = merged.astype(o_ref.dtype)


def merge_mixers(oa, ob, oc, h, w_o_a, w_o_b, w_o_c, *, tm=256):
    n = ob.shape[0]
    tm = _tile(n, tm)
    row = lambda w: pl.BlockSpec((tm, w), lambda i: (i, 0))
    gate = lambda j: pl.BlockSpec((tm, D_MODEL), lambda i: (i, GATE_OFF // D_MODEL + j))
    specs = ([row(oa.shape[1]), row(ob.shape[1]), row(oc.shape[1]), gate(0), gate(1), gate(2)]
             + [_resident(w_o_a.shape), _resident(w_o_b.shape), _resident(w_o_c.shape)])
    return pl.pallas_call(
        _merge_body, grid=(n // tm,), in_specs=specs, out_specs=row(D_MODEL),
        out_shape=jax.ShapeDtypeStruct((n, D_MODEL), BF16),
        compiler_params=_params(("parallel",)), name="merge_mixers")(oa, ob, oc, h, h, h, w_o_a, w_o_b, w_o_c)


def _proj_ln_body(a_ref, w_ref, x_ref, g_ref, b_ref, o_ref, ob_ref, *, alpha):
    y = alpha * x_ref[...] + jnp.dot(a_ref[...], w_ref[...], preferred_element_type=F32)
    out = _layer_norm(y, g_ref[...], b_ref[...])
    o_ref[...] = out
    ob_ref[...] = out.astype(BF16)


def proj_residual_ln(a, w, x, g, b, *, alpha, tm=256):
    n, d = x.shape
    tm = _tile(n, tm)
    row = lambda wd: pl.BlockSpec((tm, wd), lambda i: (i, 0))
    vec = pl.BlockSpec((1, d), lambda i: (0, 0))
    return pl.pallas_call(
        functools.partial(_proj_ln_body, alpha=alpha), grid=(n // tm,),
        in_specs=[row(a.shape[1]), _resident(w.shape), row(d), vec, vec],
        out_specs=[row(d), row(d)],
        out_shape=[jax.ShapeDtypeStruct((n, d), F32), jax.ShapeDtypeStruct((n, d), BF16)],
        compiler_params=_params(("parallel",)), name="proj_residual_ln")(a, w, x, g.reshape(1, d), b.reshape(1, d))


def _split_bf16(x):
    hi = x.astype(BF16)
    lo = (x - hi.astype(F32)).astype(BF16)
    return hi, lo


def _router_body(x_ref, wh_ref, wl_ref, bias_ref, idx_ref, wt_ref):
    xh, xl = _split_bf16(x_ref[...])
    nt = (((1,), (1,)), ((), ()))
    wh, wl = wh_ref[...], wl_ref[...]
    logits = (lax.dot_general(wh, xh, nt, preferred_element_type=F32)
              + (lax.dot_general(wh, xl, nt, preferred_element_type=F32)
                 + lax.dot_general(wl, xh, nt, preferred_element_type=F32)))
    scores = _sigmoid(logits)
    biased = scores + bias_ref[...]
    tn = scores.shape[1]
    gsz = N_EXPERTS // N_EXPERT_GROUPS
    neg = -jnp.inf
    sub = lax.broadcasted_iota(jnp.int32, (gsz, tn), 0).astype(F32)
    gs = []
    for g in range(N_EXPERT_GROUPS):
        blk = biased[g * gsz:(g + 1) * gsz]
        m1 = jnp.max(blk, 0, keepdims=True)
        first = jnp.min(jnp.where(blk == m1, sub, float(gsz)), 0, keepdims=True)
        m2 = jnp.max(jnp.where(sub == first, neg, blk), 0, keepdims=True)
        gs.append(m1 + m2)
    gs = jnp.concatenate(gs, axis=0)
    gi = lax.broadcasted_iota(jnp.int32, (N_EXPERT_GROUPS, tn), 0).astype(F32)
    gsel = jnp.zeros((N_EXPERT_GROUPS, tn), F32)
    for _ in range(TOPK_GROUPS):
        gm = jnp.max(gs, 0, keepdims=True)
        pick = gi == jnp.min(jnp.where(gs == gm, gi, float(N_EXPERT_GROUPS)), 0, keepdims=True)
        gsel = jnp.where(pick, 1.0, gsel)
        gs = jnp.where(pick, neg, gs)
    masked = jnp.concatenate(
        [jnp.where(gsel[g:g + 1] > 0.0, biased[g * gsz:(g + 1) * gsz], neg) for g in range(N_EXPERT_GROUPS)], axis=0)
    ei = lax.broadcasted_iota(jnp.int32, (N_EXPERTS, tn), 0).astype(F32)
    idxs, wts = [], []
    for _ in range(TOP_K):
        mx = jnp.max(masked, 0, keepdims=True)
        ix = jnp.min(jnp.where(masked == mx, ei, float(N_EXPERTS)), 0, keepdims=True)
        pick = ei == ix
        idxs.append(ix)
        wts.append(jnp.sum(jnp.where(pick, scores, 0.0), 0, keepdims=True))
        masked = jnp.where(pick, neg, masked)
    w = jnp.concatenate(wts, axis=0)
    idx_ref[...] = jnp.concatenate(idxs, axis=0).astype(jnp.int32)
    wt_ref[...] = w / jnp.sum(w, 0, keepdims=True) * ROUTED_SCALE


def router(x, w_router, router_bias, *, tn=512):
    n, d = x.shape
    tn = _tile(n, tn)
    wt = w_router.T.astype(F32)
    wh = wt.astype(BF16)
    wl = (wt - wh.astype(F32)).astype(BF16)
    full = lambda r, c: pl.BlockSpec((r, c), lambda i: (0, 0))
    out = pl.BlockSpec((TOP_K, tn), lambda i: (0, i))
    return pl.pallas_call(
        _router_body, grid=(n // tn,),
        in_specs=[pl.BlockSpec((tn, d), lambda i: (i, 0)), full(N_EXPERTS, d), full(N_EXPERTS, d), full(N_EXPERTS, 1)],
        out_specs=[out, out],
        out_shape=[jax.ShapeDtypeStruct((TOP_K, n), jnp.int32), jax.ShapeDtypeStruct((TOP_K, n), F32)],
        compiler_params=_params(("parallel",)), name="router")(x, wh, wl, router_bias.reshape(N_EXPERTS, 1).astype(F32))


def _silu(x):
    return x * _sigmoid(x)


def _experts_body(blk_e_ref, nused_ref, x_ref, wg_ref, wu_ref, wd_ref, o_ref, wgu_scr, wd_scr):
    i = pl.program_id(0)
    used = i < nused_ref[0]
    new_expert = blk_e_ref[i] != blk_e_ref[jnp.maximum(i - 1, 0)]

    @pl.when(used & ((i == 0) | new_expert))
    def _():
        wgu_scr[:, :EXPERT_FF] = wg_ref[...].astype(BF16)
        wgu_scr[:, EXPERT_FF:] = wu_ref[...].astype(BF16)
        wd_scr[...] = wd_ref[...].astype(BF16)

    @pl.when(used)
    def _():
        hgu = jnp.dot(x_ref[...], wgu_scr[...], preferred_element_type=F32)
        a = _silu(hgu[:, :EXPERT_FF]) * hgu[:, EXPERT_FF:]
        o_ref[...] = jnp.dot(a.astype(BF16), wd_scr[...], preferred_element_type=F32)

    @pl.when(jnp.logical_not(used))
    def _():
        o_ref[...] = jnp.zeros(o_ref.shape, o_ref.dtype)


def grouped_experts(xs, blk_e, nused, w_gate, w_up, w_down):
    p, d = xs.shape
    nb = p // MOE_ROWS
    grid_spec = pltpu.PrefetchScalarGridSpec(
        num_scalar_prefetch=2, grid=(nb,),
        in_specs=[pl.BlockSpec((MOE_ROWS, d), lambda i, e, u: (i, 0)),
                  pl.BlockSpec((None, d, EXPERT_FF), lambda i, e, u: (e[i], 0, 0)),
                  pl.BlockSpec((None, d, EXPERT_FF), lambda i, e, u: (e[i], 0, 0)),
                  pl.BlockSpec((None, EXPERT_FF, d), lambda i, e, u: (e[i], 0, 0))],
        out_specs=pl.BlockSpec((MOE_ROWS, d), lambda i, e, u: (i, 0)),
        scratch_shapes=[pltpu.VMEM((d, 2 * EXPERT_FF), BF16), pltpu.VMEM((EXPERT_FF, d), BF16)])
    return pl.pallas_call(
        _experts_body, grid_spec=grid_spec, out_shape=jax.ShapeDtypeStruct((p, d), F32),
        compiler_params=_params(("arbitrary",)), name="grouped_experts")(blk_e, nused, xs, w_gate, w_up, w_down)


def _shared_ln_body(x_ref, xb_ref, r_ref, wgu_ref, wd_ref, g_ref, b_ref, o_ref, ob_ref, *, alpha):
    hgu = jnp.dot(xb_ref[...], wgu_ref[...], preferred_element_type=F32)
    a = _silu(hgu[:, :SHARED_FF]) * hgu[:, SHARED_FF:]
    y = jnp.dot(a.astype(BF16), wd_ref[...], preferred_element_type=F32) + r_ref[...]
    out = _layer_norm(alpha * x_ref[...] + y, g_ref[...], b_ref[...])
    o_ref[...] = out
    ob_ref[...] = out.astype(BF16)


def shared_expert_ln(x, xb, routed, w_gu, w_d, g, b, *, alpha, tm=256):
    n, d = x.shape
    tm = _tile(n, tm)
    row = pl.BlockSpec((tm, d), lambda i: (i, 0))
    vec = pl.BlockSpec((1, d), lambda i: (0, 0))
    return pl.pallas_call(
        functools.partial(_shared_ln_body, alpha=alpha), grid=(n // tm,),
        in_specs=[row, row, row, _resident(w_gu.shape), _resident(w_d.shape), vec, vec],
        out_specs=[row, row],
        out_shape=[jax.ShapeDtypeStruct((n, d), F32), jax.ShapeDtypeStruct((n, d), BF16)],
        compiler_params=_params(("parallel",)), name="shared_expert_ln")(x, xb, routed, w_gu, w_d, g.reshape(1, d), b.reshape(1, d))


def _dispatch_plan(idx, n):
    a = n * TOP_K
    flat_e = idx.T.reshape(a)
    iota = jnp.arange(a, dtype=jnp.int32)
    _, order = lax.sort_key_val(flat_e, iota)
    _, rank = lax.sort_key_val(order, iota)
    experts = jnp.arange(N_EXPERTS, dtype=jnp.int32)
    counts = jnp.sum((flat_e[:, None] == experts[None, :]).astype(jnp.int32), axis=0)
    padded = (counts + MOE_ROWS - 1) // MOE_ROWS * MOE_ROWS
    pad_end = jnp.cumsum(padded)
    pad_start = pad_end - padded
    start = jnp.cumsum(counts) - counts
    n_blocks = a // MOE_ROWS + N_EXPERTS
    blk0 = jnp.arange(n_blocks, dtype=jnp.int32) * MOE_ROWS
    blk_e = jnp.minimum(jnp.sum((pad_end[None, :] <= blk0[:, None]).astype(jnp.int32), axis=1), N_EXPERTS - 1)
    src = (start - pad_start)[blk_e][:, None] + blk0[:, None] + jnp.arange(MOE_ROWS, dtype=jnp.int32)[None, :]
    slot_token = order[jnp.clip(src, 0, a - 1).reshape(-1)] // TOP_K
    nused = (pad_end[-1] // MOE_ROWS).astype(jnp.int32).reshape(1)
    slot_of = ((pad_start - start)[flat_e] + rank).reshape(n, TOP_K)
    return slot_token, blk_e.astype(jnp.int32), nused, slot_of


def _prep_layer(l, p):
    bf = lambda a: a.astype(BF16)
    w_in = p['w_in'][l]
    offs = np.cumsum((0, A_Q_HEADS * HEAD_DIM, A_KV_HEADS * HEAD_DIM, A_KV_HEADS * HEAD_DIM, B_Q_RANK, B_KV_RANK,
                      B_ROPE, C_Q_HEADS * HEAD_DIM, C_KV_HEADS * HEAD_DIM, C_KV_HEADS * HEAD_DIM, N_BRANCH * D_MODEL))
    piece = lambda i: w_in[:, offs[i]:offs[i + 1]]
    zeros = lambda c: jnp.zeros((D_MODEL, c), w_in.dtype)
    w_h = jnp.concatenate([piece(9), piece(0), piece(1), piece(2), piece(3), piece(4), piece(5), zeros(LANES - B_ROPE),
                           piece(6), piece(7), piece(8), zeros(H_COLS - H_USED)], axis=1)
    w_uq = p['w_uq'][l].reshape(B_Q_RANK, B_HEADS, B_NOPE + B_ROPE)
    w_uq_rope = jnp.pad(w_uq[:, :, B_NOPE:], ((0, 0), (0, 0), (0, LANES - B_ROPE)))
    w_q = jnp.concatenate([w_uq[:, :, :B_NOPE].reshape(B_Q_RANK, -1), w_uq_rope.reshape(B_Q_RANK, -1)], axis=1)
    w_ukv = p['w_ukv'][l].reshape(B_KV_RANK, B_HEADS, B_NOPE + B_V)
    w_k = w_ukv[:, :, :B_NOPE].reshape(B_KV_RANK, -1)
    w_v_t = w_ukv[:, :, B_NOPE:].reshape(B_KV_RANK, -1).T
    return dict(
        w_h=bf(w_h), w_q=bf(w_q), w_k=bf(w_k), w_v_t=bf(w_v_t), w_cv_t=bf(piece(8).T),
        w_o_a=bf(p['w_o_a'][l]), w_o_b=bf(p['w_o_b'][l]), w_o_c=bf(p['w_o_c'][l]), w_out=bf(p['w_out'][l]),
        w_s_gu=bf(jnp.concatenate([p['w_s_gate'][l], p['w_s_up'][l]], axis=-1)), w_s_d=bf(p['w_s_down'][l]),
        c_gains=jnp.concatenate([jnp.tile(p['c_q_norm'][l][None] * (HEAD_DIM ** -0.5 * LOG2E), (C_Q_HEADS, 1)),
                                 jnp.tile(p['c_k_norm'][l][None], (C_KV_HEADS, 1))], axis=0),
    )


def _token_mixer(x, xb, lw, l, p, tabs, segs, alpha):
    tab_a, tab_b, tab_c = tabs
    h = matmul(xb, lw['w_h'], name="in_proj")
    assert all((window // dilation) // 2 == A_RADIUS for window, dilation in A_PATTERNS)
    qk_a = rope(h, tab_a, name="rope_a", segs=segs, col=AQ_OFF, heads=A_Q_HEADS + A_KV_HEADS, shift=PARTIAL_ROT // 2)
    o_a = band_attention(qk_a, h, segs=segs)
    q_b = matmul(h, lw['w_q'], name="mla_q", x_col=BCQ_OFF, gain=p['mla_q_norm'][l],
                 out_scale=(B_NOPE + B_ROPE) ** -0.5 * LOG2E)
    k_b = matmul(h, lw['w_k'], name="mla_k", x_col=BCKV_OFF, gain=p['mla_kv_norm'][l])
    vt_b = matmul_t(h, lw['w_v_t'], name="mla_v_t", tm=FLASH_KEYS, x_col=BCKV_OFF, gain=p['mla_kv_norm'][l])
    q_b_rope = rope(q_b, tab_b, name="rope_bq", segs=segs, col=B_HEADS * LANES, heads=B_HEADS, shift=B_ROPE // 2)
    k_b_rope = rope(h, tab_b, name="rope_bk", segs=segs, col=BKR_OFF, heads=1, shift=B_ROPE // 2)
    o_b = flash_attention([(q_b, 0, LANES), (q_b_rope, 0, LANES)], [(k_b, 0, LANES), (k_b_rope, 0, 0)], vt_b,
                          segs=segs, kv_heads=B_HEADS, group=1, name="flash_b")
    qk_c = rope(h, tab_c, name="rope_c", segs=segs, col=CQ_OFF, heads=C_Q_HEADS + C_KV_HEADS, shift=HEAD_DIM // 4,
                gains=lw['c_gains'])
    vt_c = matmul_t(xb, lw['w_cv_t'], name="c_v_t", tm=FLASH_KEYS)
    group = C_Q_HEADS // C_KV_HEADS
    o_c = flash_attention([(qk_c, 0, group * LANES)], [(qk_c, C_Q_HEADS * LANES, LANES)], vt_c,
                          segs=segs, kv_heads=C_KV_HEADS, group=group, name="flash_c")
    merged = merge_mixers(o_a, o_b, o_c, h, lw['w_o_a'], lw['w_o_b'], lw['w_o_c'])
    return proj_residual_ln(merged, lw['w_out'], x, p['ln1_g'][l], p['ln1_b'][l], alpha=alpha)


def _moe(x, xb, lw, l, p, alpha):
    n = x.shape[0]
    idx, wts = router(x, p['w_router'][l], p['router_bias'][l])
    slot_token, blk_e, nused, slot_of = _dispatch_plan(idx, n)
    xs = xb.at[slot_token].get(mode='promise_in_bounds')
    y = grouped_experts(xs, blk_e, nused, p['w_e_gate'][l], p['w_e_up'][l], p['w_e_down'][l])
    routed = jnp.sum(y.at[slot_of].get(mode='promise_in_bounds') * wts.T[:, :, None], axis=1)
    return shared_expert_ln(x, xb, routed, lw['w_s_gu'], lw['w_s_d'], p['ln2_g'][l], p['ln2_b'][l], alpha=alpha)


def _run_trunk(xs3, layers, p):
    d = xs3[0].shape[-1]
    segs, row0 = [], 0
    for x3 in xs3:
        segs.append((row0, x3.shape[0], x3.shape[1]))
        row0 += x3.shape[0] * x3.shape[1]
    segs = tuple(segs)
    alpha = (2 * len(layers)) ** 0.25
    tabs = _rope_tables(max(seq for _, _, seq in segs))
    x = jnp.concatenate([x3.reshape(-1, d) for x3 in xs3], axis=0)
    xb = x.astype(BF16)
    for l, lw in enumerate(layers):
        x, xb = _token_mixer(x, xb, lw, l, p, tabs, segs, alpha)
        x, xb = _moe(x, xb, lw, l, p, alpha)
    return tuple(x[r0:r0 + b * s].reshape(b, s, d) for r0, b, s in segs)


def kernel(x_prompt, x_sample, ln1_g, ln1_b, w_in, mla_q_norm, w_uq, mla_kv_norm, w_ukv, c_q_norm, c_k_norm, w_o_a, w_o_b, w_o_c, w_out, ln2_g, ln2_b, w_router, router_bias, w_e_gate, w_e_up, w_e_down, w_s_gate, w_s_up, w_s_down):
    p = dict(ln1_g=ln1_g, ln1_b=ln1_b, w_in=w_in, mla_q_norm=mla_q_norm, w_uq=w_uq, mla_kv_norm=mla_kv_norm,
             w_ukv=w_ukv, c_q_norm=c_q_norm, c_k_norm=c_k_norm, w_o_a=w_o_a, w_o_b=w_o_b, w_o_c=w_o_c, w_out=w_out,
             ln2_g=ln2_g, ln2_b=ln2_b, w_router=w_router, router_bias=router_bias, w_e_gate=w_e_gate,
             w_e_up=w_e_up, w_e_down=w_e_down, w_s_gate=w_s_gate, w_s_up=w_s_up, w_s_down=w_s_down)
    layers = [_prep_layer(l, p) for l in range(w_in.shape[0])]
    return _run_trunk((x_prompt, x_sample), layers, p)
```

```python
import functools

import numpy as np
import jax
import jax.numpy as jnp
from jax import lax
from jax.experimental import pallas as pl
from jax.experimental.pallas import tpu as pltpu

F32 = jnp.float32
BF16 = jnp.bfloat16

D_MODEL = 2048
HEAD_DIM = 128
A_PATTERNS = ((128, 1), (512, 4), (2048, 16))
A_KV_HEADS = 4
A_Q_HEADS = len(A_PATTERNS) * A_KV_HEADS
ROPE_THETA = 500000.0
PARTIAL_ROT = HEAD_DIM // 4
B_HEADS = 8
B_Q_RANK = 512
B_KV_RANK = 256
B_NOPE = 128
B_ROPE = 64
B_V = 128
C_Q_HEADS = 8
C_KV_HEADS = 2
AXIAL_THETA = 10000.0
GRID_W = 64
N_BRANCH = 3
N_EXPERTS = 64
TOP_K = 8
N_EXPERT_GROUPS = 8
TOPK_GROUPS = 4
EXPERT_FF = 512
SHARED_FF = 512
ROUTED_SCALE = 2.5
LN_EPS = 1e-5
RMS_EPS = 1e-6
MASK_VALUE = -1e30

LANES = 128
VMEM_LIMIT = 56 * 1024 * 1024
_NT = (((1,), (1,)), ((), ()))

GATE_OFF = 0
AQ_OFF = GATE_OFF + N_BRANCH * D_MODEL
AK_OFF = AQ_OFF + A_Q_HEADS * HEAD_DIM
AV_OFF = AK_OFF + A_KV_HEADS * HEAD_DIM
BCQ_OFF = AV_OFF + A_KV_HEADS * HEAD_DIM
BCKV_OFF = BCQ_OFF + B_Q_RANK
BKR_OFF = BCKV_OFF + B_KV_RANK
CQ_OFF = BKR_OFF + LANES
CK_OFF = CQ_OFF + C_Q_HEADS * HEAD_DIM
CV_OFF = CK_OFF + C_KV_HEADS * HEAD_DIM
H_USED = CV_OFF + C_KV_HEADS * HEAD_DIM
H_COLS = -(-H_USED // 1024) * 1024

MOE_ROWS = 256


def _tile(n, t):
    t = min(n, t)
    assert n % t == 0, (n, t)
    return t


def _params(sem):
    return pltpu.CompilerParams(dimension_semantics=sem, vmem_limit_bytes=VMEM_LIMIT)


def _resident(shape):
    nd = len(shape)
    return pl.BlockSpec(shape, lambda *_: (0,) * nd, pipeline_mode=pl.Buffered(1))


def _mm_body(x_ref, w_ref, o_ref):
    o_ref[...] = jnp.dot(x_ref[...], w_ref[...], preferred_element_type=F32).astype(o_ref.dtype)


def _mm_rms_body(x_ref, g_ref, w_ref, o_ref, *, out_scale):
    xf = x_ref[...].astype(F32)
    y = xf * lax.rsqrt(jnp.mean(xf * xf, -1, keepdims=True) + RMS_EPS) * g_ref[...]
    acc = jnp.dot(y.astype(BF16), w_ref[...], preferred_element_type=F32)
    o_ref[...] = (acc * out_scale).astype(o_ref.dtype)


def matmul(x, w, *, name, x_col=0, gain=None, out_scale=1.0, tm=1024, tn=1024, out_dtype=BF16):
    m = x.shape[0]
    k, n = w.shape
    tm, tn = _tile(m, tm), _tile(n, tn)
    assert x_col % k == 0
    x_spec = pl.BlockSpec((tm, k), lambda i, j: (i, x_col // k))
    w_spec = pl.BlockSpec((k, tn), lambda i, j: (0, j))
    o_spec = pl.BlockSpec((tm, tn), lambda i, j: (i, j))
    if gain is None:
        assert out_scale == 1.0
        body, specs, args = _mm_body, [x_spec, w_spec], (x, w)
    else:
        g_spec = pl.BlockSpec((1, k), lambda i, j: (0, 0))
        body = functools.partial(_mm_rms_body, out_scale=out_scale)
        specs, args = [x_spec, g_spec, w_spec], (x, gain.reshape(1, k).astype(F32), w)
    return pl.pallas_call(
        body, grid=(m // tm, n // tn), in_specs=specs, out_specs=o_spec,
        out_shape=jax.ShapeDtypeStruct((m, n), out_dtype),
        compiler_params=_params(("parallel", "arbitrary")), name=name)(*args)


def _mm_t_body(x_ref, w_ref, o_ref):
    o_ref[...] = lax.dot_general(w_ref[...], x_ref[...], _NT, preferred_element_type=F32).astype(o_ref.dtype)


def _mm_t_rms_body(x_ref, g_ref, w_ref, o_ref):
    xf = x_ref[...].astype(F32)
    y = xf * lax.rsqrt(jnp.mean(xf * xf, -1, keepdims=True) + RMS_EPS) * g_ref[...]
    o_ref[...] = lax.dot_general(w_ref[...], y.astype(BF16), _NT, preferred_element_type=F32).astype(o_ref.dtype)


def matmul_t(x, w_t, *, name, tm, x_col=0, gain=None):
    m = x.shape[0]
    n, k = w_t.shape
    assert m % tm == 0 and x_col % k == 0
    x_spec = pl.BlockSpec((tm, k), lambda i: (i, x_col // k))
    o_spec = pl.BlockSpec((None, n, tm), lambda i: (i, 0, 0))
    if gain is None:
        body, specs, args = _mm_t_body, [x_spec, _resident(w_t.shape)], (x, w_t)
    else:
        g_spec = pl.BlockSpec((1, k), lambda i: (0, 0))
        body, specs, args = _mm_t_rms_body, [x_spec, g_spec, _resident(w_t.shape)], (x, gain.reshape(1, k).astype(F32), w_t)
    return pl.pallas_call(
        body, grid=(m // tm,), in_specs=specs, out_specs=o_spec,
        out_shape=jax.ShapeDtypeStruct((m // tm, n, tm), BF16),
        compiler_params=_params(("parallel",)), name=name)(*args)


def _rope_body(*refs, shift, rms, heads):
    if rms:
        x_ref, c_ref, s1_ref, s2_ref, g_ref, o_ref = refs
    else:
        x_ref, c_ref, s1_ref, s2_ref, o_ref = refs
    c, s1, s2 = c_ref[...], s1_ref[...], s2_ref[...]
    for hd in range(heads):
        sl = slice(hd * LANES, (hd + 1) * LANES)
        x = x_ref[:, sl].astype(F32)
        if rms:
            x = x * lax.rsqrt(jnp.mean(x * x, -1, keepdims=True) + RMS_EPS) * g_ref[hd]
        out = x * c + pltpu.roll(x, LANES - shift, 1) * s1 + pltpu.roll(x, shift, 1) * s2
        o_ref[:, sl] = out.astype(o_ref.dtype)


def rope(x, tabs, *, name, segs, col, heads, shift, gains=None, ts=1024):
    n = x.shape[0]
    ts = min([ts] + [seq for _, _, seq in segs])
    assert n % ts == 0 and all(row0 % ts == 0 and seq % ts == 0 for row0, _, seq in segs)
    blk = col // LANES

    def tab_block(r):
        idx = r * 0
        for row0, _, seq in segs:
            idx = jnp.where(r >= row0 // ts, (r - row0 // ts) % (seq // ts), idx)
        return idx

    hp = max(c for c in range(1, 9) if heads % c == 0 and blk % c == 0)
    x_spec = pl.BlockSpec((ts, hp * LANES), lambda r, h: (r, blk // hp + h))
    t_spec = pl.BlockSpec((ts, LANES), lambda r, h: (tab_block(r), 0))
    o_spec = pl.BlockSpec((ts, hp * LANES), lambda r, h: (r, h))
    specs, args = [x_spec, t_spec, t_spec, t_spec], [x, *tabs]
    if gains is not None:
        specs.append(pl.BlockSpec((hp, 1, LANES), lambda r, h: (h, 0, 0)))
        args.append(gains.reshape(heads, 1, LANES).astype(F32))
    return pl.pallas_call(
        functools.partial(_rope_body, shift=shift, rms=gains is not None, heads=hp),
        grid=(n // ts, heads // hp), in_specs=specs, out_specs=o_spec,
        out_shape=jax.ShapeDtypeStruct((n, heads * LANES), BF16),
        compiler_params=_params(("parallel", "arbitrary")), name=name)(*args)


def _rope_tables(seq):
    def table(pos, dim, theta):
        inv = theta ** (-jnp.arange(0, dim, 2, dtype=F32) / dim)
        ang = pos.astype(F32)[:, None] * inv[None, :]
        return jnp.cos(ang), jnp.sin(ang)

    pos = jnp.arange(seq)
    row = pos // GRID_W
    colp = pos % GRID_W
    one = lambda n: jnp.ones((seq, n), F32)
    zero = lambda n: jnp.zeros((seq, n), F32)
    cat = lambda *xs: jnp.concatenate(xs, -1)
    ca, sa = table(pos, PARTIAL_ROT, ROPE_THETA)
    ra = LANES - PARTIAL_ROT
    tab_a = (cat(ca, ca, one(ra)), cat(-sa, zero(LANES - 16)), cat(zero(16), sa, zero(ra)))
    cb, sb = table(pos, B_ROPE, ROPE_THETA)
    tab_b = (cat(cb, cb, one(64)), cat(-sb, zero(96)), cat(zero(32), sb, zero(64)))
    cr, sr = table(row, HEAD_DIM // 2, AXIAL_THETA)
    cc, sc = table(colp, HEAD_DIM // 2, AXIAL_THETA)
    tab_c = (cat(cr, cr, cc, cc), cat(-sr, zero(32), -sc, zero(32)), cat(zero(32), sr, zero(32), sc))
    return tab_a, tab_b, tab_c


FLASH_KEYS = 512
LOG2E = 1.4426950408889634


FLASH_STRIP = 512
ONES_ROWS = 16


def _flash_body(*refs, nq, nk, group, tq, nchunks, aliased):
    q_refs, k_refs, vt_ref = refs[:nq], refs[nq:nq + nk], refs[nq + nk]
    o_ref, qt_scr, m_scr, mblk_scr, acc_scr, s_even, s_odd = refs[nq + nk + 1 + aliased:]
    if group > 1:
        for g in range(group):
            qt_scr[:, g * tq:(g + 1) * tq] = q_refs[0][:, g * LANES:(g + 1) * LANES].astype(F32).T.astype(BF16)
    else:
        for i, r in enumerate(q_refs):
            qt_scr[i * LANES:(i + 1) * LANES, :] = r[...].astype(F32).T.astype(BF16)
    nq_all = group * tq
    ones = jnp.ones((ONES_ROWS, FLASH_KEYS), BF16)

    def keys(j):
        off = pl.multiple_of(j * FLASH_KEYS, FLASH_KEYS)
        return jnp.concatenate([r[pl.ds(off, FLASH_KEYS), :] for r in k_refs], axis=1)

    s0 = jnp.dot(keys(0), qt_scr[...], preferred_element_type=F32)
    s_even[...] = s0
    mblk_scr[...] = jnp.max(s0, 0, keepdims=True)
    m_scr[...] = jnp.full(m_scr.shape, -jnp.inf, F32)
    acc_scr[...] = jnp.zeros(acc_scr.shape, F32)

    def block(j, s_cur, s_nxt):
        k_next = keys(jnp.minimum(j + 1, nchunks - 1))
        vt1 = jnp.concatenate([vt_ref[j], ones], axis=0)
        m_prev = m_scr[...]
        m_new = jnp.maximum(m_prev, mblk_scr[...])
        alpha = jnp.exp2(m_prev - m_new)
        m_scr[...] = m_new
        for st in range(nq_all // FLASH_STRIP):
            cs = slice(st * FLASH_STRIP, (st + 1) * FLASH_STRIP)
            s_n = jnp.dot(k_next, qt_scr[:, cs], preferred_element_type=F32)
            s_nxt[:, cs] = s_n
            mblk_scr[:, cs] = jnp.max(s_n, 0, keepdims=True)
            p = jnp.exp2(s_cur[:, cs] - m_new[:, cs]).astype(BF16)
            acc_scr[:, cs] = alpha[:, cs] * acc_scr[:, cs] + jnp.dot(vt1, p, preferred_element_type=F32)

    def pair(i, carry):
        block(2 * i, s_even, s_odd)
        block(2 * i + 1, s_odd, s_even)
        return carry

    lax.fori_loop(0, nchunks // 2, pair, 0)
    acc = acc_scr[...]
    o_t = acc[:LANES] / acc[LANES:LANES + 1]
    for g in range(group):
        o_ref[:, g * LANES:(g + 1) * LANES] = o_t[:, g * tq:(g + 1) * tq].T.astype(o_ref.dtype)


def _alias_previous(specs, args, prev):
    if prev is None:
        return {}
    specs.append(pl.BlockSpec(memory_space=pl.ANY))
    args.append(prev)
    return {len(args) - 1: 0}


def flash_attention(q_srcs, k_srcs, v_t, *, segs, kv_heads, group, name, rows=1024):
    n = q_srcs[0][0].shape[0]
    nq, nk = len(q_srcs), len(k_srcs)
    qw = group * LANES
    dq = LANES if group > 1 else nq * LANES
    out = None
    for row0, batch, seq in segs:
        assert seq % (2 * FLASH_KEYS) == 0 and row0 % seq == 0
        tq = _tile(seq, rows // group)
        nqb, nchunks = seq // tq, seq // FLASH_KEYS
        r = group * tq
        assert r % FLASH_STRIP == 0
        qb0, sb0 = row0 // tq, row0 // seq
        specs, args = [], []
        for arr, col, stride in q_srcs:
            assert col % qw == 0 and stride % qw == 0
            specs.append(pl.BlockSpec((tq, qw), lambda b, h, i, c=col // qw, st=stride // qw, qb0=qb0, nqb=nqb:
                                      (qb0 + b * nqb + i, c + h * st)))
            args.append(arr)
        for arr, col, stride in k_srcs:
            specs.append(pl.BlockSpec((seq, LANES), lambda b, h, i, c=col // LANES, st=stride // LANES, sb0=sb0:
                                      (sb0 + b, c + h * st)))
            args.append(arr)
        specs.append(pl.BlockSpec((nchunks, LANES, FLASH_KEYS), lambda b, h, i, sb0=sb0: (sb0 + b, h, 0)))
        args.append(v_t)
        aliases = _alias_previous(specs, args, out)
        out = pl.pallas_call(
            functools.partial(_flash_body, nq=nq, nk=nk, group=group, tq=tq, nchunks=nchunks, aliased=out is not None),
            grid=(batch, kv_heads, nqb), in_specs=specs,
            out_specs=pl.BlockSpec((tq, qw), lambda b, h, i, qb0=qb0, nqb=nqb: (qb0 + b * nqb + i, h)),
            out_shape=jax.ShapeDtypeStruct((n, kv_heads * qw), BF16),
            scratch_shapes=[pltpu.VMEM((dq, r), BF16),
                            pltpu.VMEM((1, r), F32), pltpu.VMEM((1, r), F32),
                            pltpu.VMEM((LANES + ONES_ROWS, r), F32),
                            pltpu.VMEM((FLASH_KEYS, r), F32), pltpu.VMEM((FLASH_KEYS, r), F32)],
            input_output_aliases=aliases,
            compiler_params=_params(("parallel", "arbitrary", "arbitrary")), name=name)(*args)
    return out


A_QB = 256
A_RADIUS = 64


def _band_window(seq, dilation):
    return min(A_QB + 2 * A_RADIUS * dilation, seq)


OFF_PHASE = 1e9


def _band_body(*refs, seq, dilations, scale, aliased):
    ng = len(dilations)
    q_refs, k_ref, v_ref = refs[:ng], refs[ng], refs[ng + 1]
    rel_refs = refs[ng + 2:2 * ng + 2]
    o_ref = refs[2 * ng + 2 + aliased]
    t0 = pl.program_id(2) * A_QB
    outs, lses = [], []
    for g, d in enumerate(dilations):
        w = _band_window(seq, d)
        start = pl.multiple_of(jnp.clip(t0 - A_RADIUS * d, 0, seq - w), A_RADIUS)
        kw = k_ref[pl.ds(start, w), :]
        vw = v_ref[pl.ds(start, w), :]
        s = lax.dot_general(q_refs[g][...], kw, _NT, preferred_element_type=F32) * scale
        rel = rel_refs[g][...] + (start - t0).astype(F32)
        s = jnp.where(jnp.abs(rel) <= float(A_RADIUS * d), s, MASK_VALUE)
        m = jnp.max(s, -1, keepdims=True)
        p = jnp.exp(s - m)
        den = jnp.sum(p, -1, keepdims=True)
        outs.append(jnp.dot((p / den).astype(BF16), vw, preferred_element_type=F32))
        lses.append(m + jnp.log(den))
    mx = functools.reduce(jnp.maximum, lses)
    es = [jnp.exp(l - mx) for l in lses]
    tot = functools.reduce(lambda a, b: a + b, es)
    o_ref[...] = functools.reduce(lambda a, b: a + b, [(e / tot) * o for e, o in zip(es, outs)]).astype(o_ref.dtype)


def band_attention(qk, h, *, segs):
    n = qk.shape[0]
    dilations = tuple(d for _, d in A_PATTERNS)
    out = None
    for row0, batch, seq in segs:
        assert seq % A_QB == 0 and row0 % seq == 0
        nqb = seq // A_QB
        qb0, sb0 = row0 // A_QB, row0 // seq
        specs = [pl.BlockSpec((A_QB, LANES),
                              lambda b, hh, i, g=g, qb0=qb0, nqb=nqb: (qb0 + b * nqb + i, g * A_KV_HEADS + hh))
                 for g in range(len(dilations))]
        specs.append(pl.BlockSpec((seq, LANES), lambda b, hh, i, sb0=sb0: (sb0 + b, A_Q_HEADS + hh)))
        specs.append(pl.BlockSpec((seq, LANES), lambda b, hh, i, sb0=sb0: (sb0 + b, AV_OFF // LANES + hh)))
        args = [qk] * len(dilations) + [qk, h]
        for d in dilations:
            ji = np.arange(_band_window(seq, d))[None, :] - np.arange(A_QB)[:, None]
            rel = jnp.asarray(np.where(ji % d == 0, ji, OFF_PHASE), F32)
            specs.append(_resident(rel.shape))
            args.append(rel)
        aliases = _alias_previous(specs, args, out)
        out = pl.pallas_call(
            functools.partial(_band_body, seq=seq, dilations=dilations, scale=HEAD_DIM ** -0.5,
                              aliased=out is not None),
            grid=(batch, A_KV_HEADS, nqb), in_specs=specs,
            out_specs=pl.BlockSpec((A_QB, LANES), lambda b, hh, i, qb0=qb0, nqb=nqb: (qb0 + b * nqb + i, hh)),
            out_shape=jax.ShapeDtypeStruct((n, A_KV_HEADS * LANES), BF16),
            input_output_aliases=aliases,
            compiler_params=_params(("parallel", "arbitrary", "arbitrary")), name="band_attention")(*args)
    return out


def _sigmoid(x):
    return 1.0 / (1.0 + jnp.exp(-x))


def _layer_norm(y, g, b):
    mu = jnp.mean(y, -1, keepdims=True)
    yc = y - mu
    var = jnp.mean(yc * yc, -1, keepdims=True)
    return yc * lax.rsqrt(var + LN_EPS) * g + b


def _merge_body(oa_ref, ob_ref, oc_ref, g0_ref, g1_ref, g2_ref, wa_ref, wb_ref, wc_ref, o_ref):
    ya = jnp.dot(oa_ref[...], wa_ref[...], preferred_element_type=F32)
    yb = jnp.dot(ob_ref[...], wb_ref[...], preferred_element_type=F32)
    yc = jnp.dot(oc_ref[...], wc_ref[...], preferred_element_type=F32)
    merged = (_sigmoid(g0_ref[...].astype(F32)) * ya + _sigmoid(g1_ref[...].astype(F32)) * yb
              + _sigmoid(g2_ref[...].astype(F32)) * yc)
    o_ref[...] = merged.astype(o_ref.dtype)


def merge_mixers(oa, ob, oc, h, w_o_a, w_o_b, w_o_c, *, tm=256):
    n = ob.shape[0]
    tm = _tile(n, tm)
    row = lambda w: pl.BlockSpec((tm, w), lambda i: (i, 0))
    gate = lambda j: pl.BlockSpec((tm, D_MODEL), lambda i: (i, GATE_OFF // D_MODEL + j))
    specs = ([row(oa.shape[1]), row(ob.shape[1]), row(oc.shape[1]), gate(0), gate(1), gate(2)]
             + [_resident(w_o_a.shape), _resident(w_o_b.shape), _resident(w_o_c.shape)])
    return pl.pallas_call(
        _merge_body, grid=(n // tm,), in_specs=specs, out_specs=row(D_MODEL),
        out_shape=jax.ShapeDtypeStruct((n, D_MODEL), BF16),
        compiler_params=_params(("parallel",)), name="merge_mixers")(oa, ob, oc, h, h, h, w_o_a, w_o_b, w_o_c)


def _proj_ln_body(a_ref, w_ref, x_ref, g_ref, b_ref, o_ref, ob_ref, *, alpha):
    y = alpha * x_ref[...] + jnp.dot(a_ref[...], w_ref[...], preferred_element_type=F32)
    out = _layer_norm(y, g_ref[...], b_ref[...])
    o_ref[...] = out
    ob_ref[...] = out.astype(BF16)


def proj_residual_ln(a, w, x, g, b, *, alpha, tm=256):
    n, d = x.shape
    tm = _tile(n, tm)
    row = lambda wd: pl.BlockSpec((tm, wd), lambda i: (i, 0))
    vec = pl.BlockSpec((1, d), lambda i: (0, 0))
    return pl.pallas_call(
        functools.partial(_proj_ln_body, alpha=alpha), grid=(n // tm,),
        in_specs=[row(a.shape[1]), _resident(w.shape), row(d), vec, vec],
        out_specs=[row(d), row(d)],
        out_shape=[jax.ShapeDtypeStruct((n, d), F32), jax.ShapeDtypeStruct((n, d), BF16)],
        compiler_params=_params(("parallel",)), name="proj_residual_ln")(a, w, x, g.reshape(1, d), b.reshape(1, d))


def _split_bf16(x):
    hi = x.astype(BF16)
    lo = (x - hi.astype(F32)).astype(BF16)
    return hi, lo


def _router_body(x_ref, wh_ref, wl_ref, bias_ref, idx_ref, wt_ref):
    xh, xl = _split_bf16(x_ref[...])
    nt = (((1,), (1,)), ((), ()))
    wh, wl = wh_ref[...], wl_ref[...]
    logits = (lax.dot_general(wh, xh, nt, preferred_element_type=F32)
              + (lax.dot_general(wh, xl, nt, preferred_element_type=F32)
                 + lax.dot_general(wl, xh, nt, preferred_element_type=F32)))
    scores = _sigmoid(logits)
    biased = scores + bias_ref[...]
    tn = scores.shape[1]
    gsz = N_EXPERTS // N_EXPERT_GROUPS
    neg = -jnp.inf
    sub = lax.broadcasted_iota(jnp.int32, (gsz, tn), 0).astype(F32)
    gs = []
    for g in range(N_EXPERT_GROUPS):
        blk = biased[g * gsz:(g + 1) * gsz]
        m1 = jnp.max(blk, 0, keepdims=True)
        first = jnp.min(jnp.where(blk == m1, sub, float(gsz)), 0, keepdims=True)
        m2 = jnp.max(jnp.where(sub == first, neg, blk), 0, keepdims=True)
        gs.append(m1 + m2)
    gs = jnp.concatenate(gs, axis=0)
    gi = lax.broadcasted_iota(jnp.int32, (N_EXPERT_GROUPS, tn), 0).astype(F32)
    gsel = jnp.zeros((N_EXPERT_GROUPS, tn), F32)
    for _ in range(TOPK_GROUPS):
        gm = jnp.max(gs, 0, keepdims=True)
        pick = gi == jnp.min(jnp.where(gs == gm, gi, float(N_EXPERT_GROUPS)), 0, keepdims=True)
        gsel = jnp.where(pick, 1.0, gsel)
        gs = jnp.where(pick, neg, gs)
    masked = jnp.concatenate(
        [jnp.where(gsel[g:g + 1] > 0.0, biased[g * gsz:(g + 1) * gsz], neg) for g in range(N_EXPERT_GROUPS)], axis=0)
    ei = lax.broadcasted_iota(jnp.int32, (N_EXPERTS, tn), 0).astype(F32)
    idxs, wts = [], []
    for _ in range(TOP_K):
        mx = jnp.max(masked, 0, keepdims=True)
        ix = jnp.min(jnp.where(masked == mx, ei, float(N_EXPERTS)), 0, keepdims=True)
        pick = ei == ix
        idxs.append(ix)
        wts.append(jnp.sum(jnp.where(pick, scores, 0.0), 0, keepdims=True))
        masked = jnp.where(pick, neg, masked)
    w = jnp.concatenate(wts, axis=0)
    idx_ref[...] = jnp.concatenate(idxs, axis=0).astype(jnp.int32)
    wt_ref[...] = w / jnp.sum(w, 0, keepdims=True) * ROUTED_SCALE


def router(x, w_router, router_bias, *, tn=512):
    n, d = x.shape
    tn = _tile(n, tn)
    wt = w_router.T.astype(F32)
    wh = wt.astype(BF16)
    wl = (wt - wh.astype(F32)).astype(BF16)
    full = lambda r, c: pl.BlockSpec((r, c), lambda i: (0, 0))
    out = pl.BlockSpec((TOP_K, tn), lambda i: (0, i))
    return pl.pallas_call(
        _router_body, grid=(n // tn,),
        in_specs=[pl.BlockSpec((tn, d), lambda i: (i, 0)), full(N_EXPERTS, d), full(N_EXPERTS, d), full(N_EXPERTS, 1)],
        out_specs=[out, out],
        out_shape=[jax.ShapeDtypeStruct((TOP_K, n), jnp.int32), jax.ShapeDtypeStruct((TOP_K, n), F32)],
        compiler_params=_params(("parallel",)), name="router")(x, wh, wl, router_bias.reshape(N_EXPERTS, 1).astype(F32))


def _silu(x):
    return x * _sigmoid(x)


def _experts_body(blk_e_ref, nused_ref, x_ref, wg_ref, wu_ref, wd_ref, o_ref, wgu_scr, wd_scr):
    i = pl.program_id(0)
    used = i < nused_ref[0]
    new_expert = blk_e_ref[i] != blk_e_ref[jnp.maximum(i - 1, 0)]

    @pl.when(used & ((i == 0) | new_expert))
    def _():
        wgu_scr[:, :EXPERT_FF] = wg_ref[...].astype(BF16)
        wgu_scr[:, EXPERT_FF:] = wu_ref[...].astype(BF16)
        wd_scr[...] = wd_ref[...].astype(BF16)

    @pl.when(used)
    def _():
        hgu = jnp.dot(x_ref[...], wgu_scr[...], preferred_element_type=F32)
        a = _silu(hgu[:, :EXPERT_FF]) * hgu[:, EXPERT_FF:]
        o_ref[...] = jnp.dot(a.astype(BF16), wd_scr[...], preferred_element_type=F32).astype(o_ref.dtype)

    @pl.when(jnp.logical_not(used))
    def _():
        o_ref[...] = jnp.zeros(o_ref.shape, o_ref.dtype)


def grouped_experts(xs, blk_e, nused, w_gate, w_up, w_down):
    p, d = xs.shape
    nb = p // MOE_ROWS
    grid_spec = pltpu.PrefetchScalarGridSpec(
        num_scalar_prefetch=2, grid=(nb,),
        in_specs=[pl.BlockSpec((MOE_ROWS, d), lambda i, e, u: (i, 0)),
                  pl.BlockSpec((None, d, EXPERT_FF), lambda i, e, u: (e[i], 0, 0)),
                  pl.BlockSpec((None, d, EXPERT_FF), lambda i, e, u: (e[i], 0, 0)),
                  pl.BlockSpec((None, EXPERT_FF, d), lambda i, e, u: (e[i], 0, 0))],
        out_specs=pl.BlockSpec((MOE_ROWS, d), lambda i, e, u: (i, 0)),
        scratch_shapes=[pltpu.VMEM((d, 2 * EXPERT_FF), BF16), pltpu.VMEM((EXPERT_FF, d), BF16)])
    return pl.pallas_call(
        _experts_body, grid_spec=grid_spec, out_shape=jax.ShapeDtypeStruct((p, d), BF16),
        compiler_params=_params(("arbitrary",)), name="grouped_experts")(blk_e, nused, xs, w_gate, w_up, w_down)


def _shared_ln_body(x_ref, xb_ref, r_ref, wgu_ref, wd_ref, g_ref, b_ref, o_ref, *ob_ref, alpha):
    hgu = jnp.dot(xb_ref[...], wgu_ref[...], preferred_element_type=F32)
    a = _silu(hgu[:, :SHARED_FF]) * hgu[:, SHARED_FF:]
    y = jnp.dot(a.astype(BF16), wd_ref[...], preferred_element_type=F32) + r_ref[...]
    out = _layer_norm(alpha * x_ref[...] + y, g_ref[...], b_ref[...])
    o_ref[...] = out
    for ref in ob_ref:
        ref[...] = out.astype(BF16)


def shared_expert_ln(x, xb, routed, w_gu, w_d, g, b, *, alpha, rows=None, with_bf16=True, tm=256):
    d = x.shape[1]
    row0, n = (0, x.shape[0]) if rows is None else rows
    tm = _tile(n, tm)
    assert row0 % tm == 0
    src = pl.BlockSpec((tm, d), lambda i: (row0 // tm + i, 0))
    dst = pl.BlockSpec((tm, d), lambda i: (i, 0))
    vec = pl.BlockSpec((1, d), lambda i: (0, 0))
    out_shape = [jax.ShapeDtypeStruct((n, d), F32)] + [jax.ShapeDtypeStruct((n, d), BF16)] * with_bf16
    return pl.pallas_call(
        functools.partial(_shared_ln_body, alpha=alpha), grid=(n // tm,),
        in_specs=[src, src, src, _resident(w_gu.shape), _resident(w_d.shape), vec, vec],
        out_specs=[dst] * len(out_shape), out_shape=out_shape,
        compiler_params=_params(("parallel",)), name="shared_expert_ln")(x, xb, routed, w_gu, w_d, g.reshape(1, d), b.reshape(1, d))


def _dispatch_plan(idx, n):
    a = n * TOP_K
    flat_e = idx.T.reshape(a)
    iota = jnp.arange(a, dtype=jnp.int32)
    _, order = lax.sort_key_val(flat_e, iota)
    _, rank = lax.sort_key_val(order, iota)
    experts = jnp.arange(N_EXPERTS, dtype=jnp.int32)
    counts = jnp.sum((flat_e[:, None] == experts[None, :]).astype(jnp.int32), axis=0)
    padded = (counts + MOE_ROWS - 1) // MOE_ROWS * MOE_ROWS
    pad_end = jnp.cumsum(padded)
    pad_start = pad_end - padded
    start = jnp.cumsum(counts) - counts
    n_blocks = a // MOE_ROWS + N_EXPERTS
    blk0 = jnp.arange(n_blocks, dtype=jnp.int32) * MOE_ROWS
    blk_e = jnp.minimum(jnp.sum((pad_end[None, :] <= blk0[:, None]).astype(jnp.int32), axis=1), N_EXPERTS - 1)
    src = (start - pad_start)[blk_e][:, None] + blk0[:, None] + jnp.arange(MOE_ROWS, dtype=jnp.int32)[None, :]
    slot_token = order[jnp.clip(src, 0, a - 1).reshape(-1)] // TOP_K
    nused = (pad_end[-1] // MOE_ROWS).astype(jnp.int32).reshape(1)
    slot_of = ((pad_start - start)[flat_e] + rank).reshape(n, TOP_K)
    return slot_token, blk_e.astype(jnp.int32), nused, slot_of


def _prep_layer(l, p):
    bf = lambda a: a.astype(BF16)
    w_in = p['w_in'][l]
    offs = np.cumsum((0, A_Q_HEADS * HEAD_DIM, A_KV_HEADS * HEAD_DIM, A_KV_HEADS * HEAD_DIM, B_Q_RANK, B_KV_RANK,
                      B_ROPE, C_Q_HEADS * HEAD_DIM, C_KV_HEADS * HEAD_DIM, C_KV_HEADS * HEAD_DIM, N_BRANCH * D_MODEL))
    piece = lambda i: w_in[:, offs[i]:offs[i + 1]]
    zeros = lambda c: jnp.zeros((D_MODEL, c), w_in.dtype)
    w_h = jnp.concatenate([piece(9), piece(0), piece(1), piece(2), piece(3), piece(4), piece(5), zeros(LANES - B_ROPE),
                           piece(6), piece(7), piece(8), zeros(H_COLS - H_USED)], axis=1)
    w_uq = p['w_uq'][l].reshape(B_Q_RANK, B_HEADS, B_NOPE + B_ROPE)
    w_uq_rope = jnp.pad(w_uq[:, :, B_NOPE:], ((0, 0), (0, 0), (0, LANES - B_ROPE)))
    w_q = jnp.concatenate([w_uq[:, :, :B_NOPE].reshape(B_Q_RANK, -1), w_uq_rope.reshape(B_Q_RANK, -1)], axis=1)
    w_ukv = p['w_ukv'][l].reshape(B_KV_RANK, B_HEADS, B_NOPE + B_V)
    w_k = w_ukv[:, :, :B_NOPE].reshape(B_KV_RANK, -1)
    w_v_t = w_ukv[:, :, B_NOPE:].reshape(B_KV_RANK, -1).T
    return dict(
        w_h=bf(w_h), w_q=bf(w_q), w_k=bf(w_k), w_v_t=bf(w_v_t), w_cv_t=bf(piece(8).T),
        w_o_a=bf(p['w_o_a'][l]), w_o_b=bf(p['w_o_b'][l]), w_o_c=bf(p['w_o_c'][l]), w_out=bf(p['w_out'][l]),
        w_s_gu=bf(jnp.concatenate([p['w_s_gate'][l], p['w_s_up'][l]], axis=-1)), w_s_d=bf(p['w_s_down'][l]),
        c_gains=jnp.concatenate([jnp.tile(p['c_q_norm'][l][None] * (HEAD_DIM ** -0.5 * LOG2E), (C_Q_HEADS, 1)),
                                 jnp.tile(p['c_k_norm'][l][None], (C_KV_HEADS, 1))], axis=0),
    )


def _token_mixer(x, xb, lw, l, p, tabs, segs, alpha):
    tab_a, tab_b, tab_c = tabs
    h = matmul(xb, lw['w_h'], name="in_proj")
    assert all((window // dilation) // 2 == A_RADIUS for window, dilation in A_PATTERNS)
    qk_a = rope(h, tab_a, name="rope_a", segs=segs, col=AQ_OFF, heads=A_Q_HEADS + A_KV_HEADS, shift=PARTIAL_ROT // 2)
    o_a = band_attention(qk_a, h, segs=segs)
    q_b = matmul(h, lw['w_q'], name="mla_q", x_col=BCQ_OFF, gain=p['mla_q_norm'][l],
                 out_scale=(B_NOPE + B_ROPE) ** -0.5 * LOG2E)
    k_b = matmul(h, lw['w_k'], name="mla_k", x_col=BCKV_OFF, gain=p['mla_kv_norm'][l])
    vt_b = matmul_t(h, lw['w_v_t'], name="mla_v_t", tm=FLASH_KEYS, x_col=BCKV_OFF, gain=p['mla_kv_norm'][l])
    q_b_rope = rope(q_b, tab_b, name="rope_bq", segs=segs, col=B_HEADS * LANES, heads=B_HEADS, shift=B_ROPE // 2)
    k_b_rope = rope(h, tab_b, name="rope_bk", segs=segs, col=BKR_OFF, heads=1, shift=B_ROPE // 2)
    o_b = flash_attention([(q_b, 0, LANES), (q_b_rope, 0, LANES)], [(k_b, 0, LANES), (k_b_rope, 0, 0)], vt_b,
                          segs=segs, kv_heads=B_HEADS, group=1, name="flash_b")
    qk_c = rope(h, tab_c, name="rope_c", segs=segs, col=CQ_OFF, heads=C_Q_HEADS + C_KV_HEADS, shift=HEAD_DIM // 4,
                gains=lw['c_gains'])
    vt_c = matmul_t(xb, lw['w_cv_t'], name="c_v_t", tm=FLASH_KEYS)
    group = C_Q_HEADS // C_KV_HEADS
    o_c = flash_attention([(qk_c, 0, group * LANES)], [(qk_c, C_Q_HEADS * LANES, LANES)], vt_c,
                          segs=segs, kv_heads=C_KV_HEADS, group=group, name="flash_c")
    merged = merge_mixers(o_a, o_b, o_c, h, lw['w_o_a'], lw['w_o_b'], lw['w_o_c'])
    return proj_residual_ln(merged, lw['w_out'], x, p['ln1_g'][l], p['ln1_b'][l], alpha=alpha)


def _moe(x, xb, lw, l, p, alpha, out_rows=None):
    n = x.shape[0]
    idx, wts = router(x, p['w_router'][l], p['router_bias'][l])
    slot_token, blk_e, nused, slot_of = _dispatch_plan(idx, n)
    xs = xb.at[slot_token].get(mode='promise_in_bounds')
    y = grouped_experts(xs, blk_e, nused, p['w_e_gate'][l], p['w_e_up'][l], p['w_e_down'][l])
    routed = jnp.sum(y.at[slot_of].get(mode='promise_in_bounds').astype(F32) * wts.T[:, :, None], axis=1)
    final = functools.partial(shared_expert_ln, x, xb, routed, lw['w_s_gu'], lw['w_s_d'], p['ln2_g'][l], p['ln2_b'][l],
                              alpha=alpha)
    if out_rows is None:
        return final()
    return tuple(final(rows=rows, with_bf16=False)[0] for rows in out_rows)


def _run_trunk(xs3, layers, p):
    d = xs3[0].shape[-1]
    segs, row0 = [], 0
    for x3 in xs3:
        segs.append((row0, x3.shape[0], x3.shape[1]))
        row0 += x3.shape[0] * x3.shape[1]
    segs = tuple(segs)
    alpha = (2 * len(layers)) ** 0.25
    tabs = _rope_tables(max(seq for _, _, seq in segs))
    x = jnp.concatenate([x3.reshape(-1, d) for x3 in xs3], axis=0)
    xb = x.astype(BF16)
    last = len(layers) - 1
    for l, lw in enumerate(layers):
        x, xb = _token_mixer(x, xb, lw, l, p, tabs, segs, alpha)
        if l < last:
            x, xb = _moe(x, xb, lw, l, p, alpha)
    outs = _moe(x, xb, layers[last], last, p, alpha, out_rows=tuple((r0, b * s) for r0, b, s in segs))
    return tuple(o.reshape(b, s, d) for o, (_, b, s) in zip(outs, segs))


def kernel(x_prompt, x_sample, ln1_g, ln1_b, w_in, mla_q_norm, w_uq, mla_kv_norm, w_ukv, c_q_norm, c_k_norm, w_o_a, w_o_b, w_o_c, w_out, ln2_g, ln2_b, w_router, router_bias, w_e_gate, w_e_up, w_e_down, w_s_gate, w_s_up, w_s_down):
    p = dict(ln1_g=ln1_g, ln1_b=ln1_b, w_in=w_in, mla_q_norm=mla_q_norm, w_uq=w_uq, mla_kv_norm=mla_kv_norm,
             w_ukv=w_ukv, c_q_norm=c_q_norm, c_k_norm=c_k_norm, w_o_a=w_o_a, w_o_b=w_o_b, w_o_c=w_o_c, w_out=w_out,
             ln2_g=ln2_g, ln2_b=ln2_b, w_router=w_router, router_bias=router_bias, w_e_gate=w_e_gate,
             w_e_up=w_e_up, w_e_down=w_e_down, w_s_gate=w_s_gate, w_s_up=w_s_up, w_s_down=w_s_down)
    layers = [_prep_layer(l, p) for l in range(w_in.shape[0])]
    return _run_trunk((x_prompt, x_sample), layers, p)
```

```python
import functools

import numpy as np
import jax
import jax.numpy as jnp
from jax import lax
from jax.experimental import pallas as pl
from jax.experimental.pallas import tpu as pltpu

F32 = jnp.float32
BF16 = jnp.bfloat16

D_MODEL = 2048
HEAD_DIM = 128
A_PATTERNS = ((128, 1), (512, 4), (2048, 16))
A_KV_HEADS = 4
A_Q_HEADS = len(A_PATTERNS) * A_KV_HEADS
ROPE_THETA = 500000.0
PARTIAL_ROT = HEAD_DIM // 4
B_HEADS = 8
B_Q_RANK = 512
B_KV_RANK = 256
B_NOPE = 128
B_ROPE = 64
B_V = 128
C_Q_HEADS = 8
C_KV_HEADS = 2
AXIAL_THETA = 10000.0
GRID_W = 64
N_BRANCH = 3
N_EXPERTS = 64
TOP_K = 8
N_EXPERT_GROUPS = 8
TOPK_GROUPS = 4
EXPERT_FF = 512
SHARED_FF = 512
ROUTED_SCALE = 2.5
LN_EPS = 1e-5
RMS_EPS = 1e-6
MASK_VALUE = -1e30

LANES = 128
VMEM_LIMIT = 56 * 1024 * 1024
_NT = (((1,), (1,)), ((), ()))

GATE_OFF = 0
AQ_OFF = GATE_OFF + N_BRANCH * D_MODEL
AK_OFF = AQ_OFF + A_Q_HEADS * HEAD_DIM
AV_OFF = AK_OFF + A_KV_HEADS * HEAD_DIM
BCQ_OFF = AV_OFF + A_KV_HEADS * HEAD_DIM
BCKV_OFF = BCQ_OFF + B_Q_RANK
BKR_OFF = BCKV_OFF + B_KV_RANK
CQ_OFF = BKR_OFF + LANES
CK_OFF = CQ_OFF + C_Q_HEADS * HEAD_DIM
CV_OFF = CK_OFF + C_KV_HEADS * HEAD_DIM
H_USED = CV_OFF + C_KV_HEADS * HEAD_DIM
H_COLS = -(-H_USED // 1024) * 1024

MOE_ROWS = 256


def _tile(n, t):
    t = min(n, t)
    assert n % t == 0, (n, t)
    return t


def _params(sem):
    return pltpu.CompilerParams(dimension_semantics=sem, vmem_limit_bytes=VMEM_LIMIT)


def _resident(shape):
    nd = len(shape)
    return pl.BlockSpec(shape, lambda *_: (0,) * nd, pipeline_mode=pl.Buffered(1))


def _mm_body(x_ref, w_ref, o_ref):
    o_ref[...] = jnp.dot(x_ref[...], w_ref[...], preferred_element_type=F32).astype(o_ref.dtype)


def _mm_rms_body(x_ref, g_ref, w_ref, o_ref, *, out_scale):
    xf = x_ref[...].astype(F32)
    y = xf * lax.rsqrt(jnp.mean(xf * xf, -1, keepdims=True) + RMS_EPS) * g_ref[...]
    acc = jnp.dot(y.astype(BF16), w_ref[...], preferred_element_type=F32)
    o_ref[...] = (acc * out_scale).astype(o_ref.dtype)


def matmul(x, w, *, name, x_col=0, gain=None, out_scale=1.0, tm=1024, tn=1024, out_dtype=BF16):
    m = x.shape[0]
    k, n = w.shape
    tm, tn = _tile(m, tm), _tile(n, tn)
    assert x_col % k == 0
    x_spec = pl.BlockSpec((tm, k), lambda i, j: (i, x_col // k))
    w_spec = pl.BlockSpec((k, tn), lambda i, j: (0, j))
    o_spec = pl.BlockSpec((tm, tn), lambda i, j: (i, j))
    if gain is None:
        assert out_scale == 1.0
        body, specs, args = _mm_body, [x_spec, w_spec], (x, w)
    else:
        g_spec = pl.BlockSpec((1, k), lambda i, j: (0, 0))
        body = functools.partial(_mm_rms_body, out_scale=out_scale)
        specs, args = [x_spec, g_spec, w_spec], (x, gain.reshape(1, k).astype(F32), w)
    return pl.pallas_call(
        body, grid=(m // tm, n // tn), in_specs=specs, out_specs=o_spec,
        out_shape=jax.ShapeDtypeStruct((m, n), out_dtype),
        compiler_params=_params(("parallel", "arbitrary")), name=name)(*args)


def _mm_t_body(x_ref, w_ref, o_ref):
    o_ref[...] = lax.dot_general(w_ref[...], x_ref[...], _NT, preferred_element_type=F32).astype(o_ref.dtype)


def _mm_t_rms_body(x_ref, g_ref, w_ref, o_ref):
    xf = x_ref[...].astype(F32)
    y = xf * lax.rsqrt(jnp.mean(xf * xf, -1, keepdims=True) + RMS_EPS) * g_ref[...]
    o_ref[...] = lax.dot_general(w_ref[...], y.astype(BF16), _NT, preferred_element_type=F32).astype(o_ref.dtype)


def matmul_t(x, w_t, *, name, tm, x_col=0, gain=None):
    m = x.shape[0]
    n, k = w_t.shape
    assert m % tm == 0 and x_col % k == 0
    x_spec = pl.BlockSpec((tm, k), lambda i: (i, x_col // k))
    o_spec = pl.BlockSpec((None, n, tm), lambda i: (i, 0, 0))
    if gain is None:
        body, specs, args = _mm_t_body, [x_spec, _resident(w_t.shape)], (x, w_t)
    else:
        g_spec = pl.BlockSpec((1, k), lambda i: (0, 0))
        body, specs, args = _mm_t_rms_body, [x_spec, g_spec, _resident(w_t.shape)], (x, gain.reshape(1, k).astype(F32), w_t)
    return pl.pallas_call(
        body, grid=(m // tm,), in_specs=specs, out_specs=o_spec,
        out_shape=jax.ShapeDtypeStruct((m // tm, n, tm), BF16),
        compiler_params=_params(("parallel",)), name=name)(*args)


def _rope_body(*refs, shift, rms, heads):
    if rms:
        x_ref, c_ref, s1_ref, s2_ref, g_ref, o_ref = refs
    else:
        x_ref, c_ref, s1_ref, s2_ref, o_ref = refs
    c, s1, s2 = c_ref[...], s1_ref[...], s2_ref[...]
    for hd in range(heads):
        sl = slice(hd * LANES, (hd + 1) * LANES)
        x = x_ref[:, sl].astype(F32)
        if rms:
            x = x * lax.rsqrt(jnp.mean(x * x, -1, keepdims=True) + RMS_EPS) * g_ref[hd]
        out = x * c + pltpu.roll(x, LANES - shift, 1) * s1 + pltpu.roll(x, shift, 1) * s2
        o_ref[:, sl] = out.astype(o_ref.dtype)


def rope(x, tabs, *, name, segs, col, heads, shift, gains=None, ts=1024):
    n = x.shape[0]
    ts = min([ts] + [seq for _, _, seq in segs])
    assert n % ts == 0 and all(row0 % ts == 0 and seq % ts == 0 for row0, _, seq in segs)
    blk = col // LANES

    def tab_block(r):
        idx = r * 0
        for row0, _, seq in segs:
            idx = jnp.where(r >= row0 // ts, (r - row0 // ts) % (seq // ts), idx)
        return idx

    hp = max(c for c in range(1, 9) if heads % c == 0 and blk % c == 0)
    x_spec = pl.BlockSpec((ts, hp * LANES), lambda r, h: (r, blk // hp + h))
    t_spec = pl.BlockSpec((ts, LANES), lambda r, h: (tab_block(r), 0))
    o_spec = pl.BlockSpec((ts, hp * LANES), lambda r, h: (r, h))
    specs, args = [x_spec, t_spec, t_spec, t_spec], [x, *tabs]
    if gains is not None:
        specs.append(pl.BlockSpec((hp, 1, LANES), lambda r, h: (h, 0, 0)))
        args.append(gains.reshape(heads, 1, LANES).astype(F32))
    return pl.pallas_call(
        functools.partial(_rope_body, shift=shift, rms=gains is not None, heads=hp),
        grid=(n // ts, heads // hp), in_specs=specs, out_specs=o_spec,
        out_shape=jax.ShapeDtypeStruct((n, heads * LANES), BF16),
        compiler_params=_params(("parallel", "arbitrary")), name=name)(*args)


def _rope_tables(seq):
    def table(pos, dim, theta):
        inv = theta ** (-jnp.arange(0, dim, 2, dtype=F32) / dim)
        ang = pos.astype(F32)[:, None] * inv[None, :]
        return jnp.cos(ang), jnp.sin(ang)

    pos = jnp.arange(seq)
    row = pos // GRID_W
    colp = pos % GRID_W
    one = lambda n: jnp.ones((seq, n), F32)
    zero = lambda n: jnp.zeros((seq, n), F32)
    cat = lambda *xs: jnp.concatenate(xs, -1)
    ca, sa = table(pos, PARTIAL_ROT, ROPE_THETA)
    ra = LANES - PARTIAL_ROT
    tab_a = (cat(ca, ca, one(ra)), cat(-sa, zero(LANES - 16)), cat(zero(16), sa, zero(ra)))
    cb, sb = table(pos, B_ROPE, ROPE_THETA)
    tab_b = (cat(cb, cb, one(64)), cat(-sb, zero(96)), cat(zero(32), sb, zero(64)))
    cr, sr = table(row, HEAD_DIM // 2, AXIAL_THETA)
    cc, sc = table(colp, HEAD_DIM // 2, AXIAL_THETA)
    tab_c = (cat(cr, cr, cc, cc), cat(-sr, zero(32), -sc, zero(32)), cat(zero(32), sr, zero(32), sc))
    return tab_a, tab_b, tab_c


FLASH_KEYS = 512
LOG2E = 1.4426950408889634


FLASH_STRIP = 512
ONES_ROWS = 16


def _flash_body(*refs, nq, nk, group, tq, nchunks, aliased):
    q_refs, k_refs, vt_ref = refs[:nq], refs[nq:nq + nk], refs[nq + nk]
    o_ref, qt_scr, m_scr, mblk_scr, acc_scr, s_even, s_odd = refs[nq + nk + 1 + aliased:]
    if group > 1:
        for g in range(group):
            qt_scr[:, g * tq:(g + 1) * tq] = q_refs[0][:, g * LANES:(g + 1) * LANES].astype(F32).T.astype(BF16)
    else:
        for i, r in enumerate(q_refs):
            qt_scr[i * LANES:(i + 1) * LANES, :] = r[...].astype(F32).T.astype(BF16)
    nq_all = group * tq
    ones = jnp.ones((ONES_ROWS, FLASH_KEYS), BF16)

    def keys(j):
        off = pl.multiple_of(j * FLASH_KEYS, FLASH_KEYS)
        return jnp.concatenate([r[pl.ds(off, FLASH_KEYS), :] for r in k_refs], axis=1)

    s0 = jnp.dot(keys(0), qt_scr[...], preferred_element_type=F32)
    s_even[...] = s0
    mblk_scr[...] = jnp.max(s0, 0, keepdims=True)
    m_scr[...] = jnp.full(m_scr.shape, -jnp.inf, F32)
    acc_scr[...] = jnp.zeros(acc_scr.shape, F32)

    def block(j, s_cur, s_nxt):
        k_next = keys(jnp.minimum(j + 1, nchunks - 1))
        vt1 = jnp.concatenate([vt_ref[j], ones], axis=0)
        m_prev = m_scr[...]
        m_new = jnp.maximum(m_prev, mblk_scr[...])
        alpha = jnp.exp2(m_prev - m_new)
        m_scr[...] = m_new
        for st in range(nq_all // FLASH_STRIP):
            cs = slice(st * FLASH_STRIP, (st + 1) * FLASH_STRIP)
            s_n = jnp.dot(k_next, qt_scr[:, cs], preferred_element_type=F32)
            s_nxt[:, cs] = s_n
            mblk_scr[:, cs] = jnp.max(s_n, 0, keepdims=True)
            p = jnp.exp2(s_cur[:, cs] - m_new[:, cs]).astype(BF16)
            acc_scr[:, cs] = alpha[:, cs] * acc_scr[:, cs] + jnp.dot(vt1, p, preferred_element_type=F32)

    def pair(i, carry):
        block(2 * i, s_even, s_odd)
        block(2 * i + 1, s_odd, s_even)
        return carry

    lax.fori_loop(0, nchunks // 2, pair, 0)
    acc = acc_scr[...]
    o_t = acc[:LANES] / acc[LANES:LANES + 1]
    for g in range(group):
        o_ref[:, g * LANES:(g + 1) * LANES] = o_t[:, g * tq:(g + 1) * tq].T.astype(o_ref.dtype)


def _alias_previous(specs, args, prev):
    if prev is None:
        return {}
    specs.append(pl.BlockSpec(memory_space=pl.ANY))
    args.append(prev)
    return {len(args) - 1: 0}


def flash_attention(q_srcs, k_srcs, v_t, *, segs, kv_heads, group, name, rows=1024):
    n = q_srcs[0][0].shape[0]
    nq, nk = len(q_srcs), len(k_srcs)
    qw = group * LANES
    dq = LANES if group > 1 else nq * LANES
    out = None
    for row0, batch, seq in segs:
        assert seq % (2 * FLASH_KEYS) == 0 and row0 % seq == 0
        tq = _tile(seq, rows // group)
        nqb, nchunks = seq // tq, seq // FLASH_KEYS
        r = group * tq
        assert r % FLASH_STRIP == 0
        qb0, sb0 = row0 // tq, row0 // seq
        specs, args = [], []
        for arr, col, stride in q_srcs:
            assert col % qw == 0 and stride % qw == 0
            specs.append(pl.BlockSpec((tq, qw), lambda b, h, i, c=col // qw, st=stride // qw, qb0=qb0, nqb=nqb:
                                      (qb0 + b * nqb + i, c + h * st)))
            args.append(arr)
        for arr, col, stride in k_srcs:
            specs.append(pl.BlockSpec((seq, LANES), lambda b, h, i, c=col // LANES, st=stride // LANES, sb0=sb0:
                                      (sb0 + b, c + h * st)))
            args.append(arr)
        specs.append(pl.BlockSpec((nchunks, LANES, FLASH_KEYS), lambda b, h, i, sb0=sb0: (sb0 + b, h, 0)))
        args.append(v_t)
        aliases = _alias_previous(specs, args, out)
        out = pl.pallas_call(
            functools.partial(_flash_body, nq=nq, nk=nk, group=group, tq=tq, nchunks=nchunks, aliased=out is not None),
            grid=(batch, kv_heads, nqb), in_specs=specs,
            out_specs=pl.BlockSpec((tq, qw), lambda b, h, i, qb0=qb0, nqb=nqb: (qb0 + b * nqb + i, h)),
            out_shape=jax.ShapeDtypeStruct((n, kv_heads * qw), BF16),
            scratch_shapes=[pltpu.VMEM((dq, r), BF16),
                            pltpu.VMEM((1, r), F32), pltpu.VMEM((1, r), F32),
                            pltpu.VMEM((LANES + ONES_ROWS, r), F32),
                            pltpu.VMEM((FLASH_KEYS, r), F32), pltpu.VMEM((FLASH_KEYS, r), F32)],
            input_output_aliases=aliases,
            compiler_params=_params(("parallel", "arbitrary", "arbitrary")), name=name)(*args)
    return out


A_QB = 256
A_RADIUS = 64


def _band_window(seq, dilation):
    return min(A_QB + 2 * A_RADIUS * dilation, seq)


OFF_PHASE = 1e9


def _band_body(*refs, seq, dilations, scale, aliased):
    ng = len(dilations)
    q_refs, k_ref, v_ref = refs[:ng], refs[ng], refs[ng + 1]
    rel_refs = refs[ng + 2:2 * ng + 2]
    o_ref = refs[2 * ng + 2 + aliased]
    t0 = pl.program_id(2) * A_QB
    outs, lses = [], []
    for g, d in enumerate(dilations):
        w = _band_window(seq, d)
        start = pl.multiple_of(jnp.clip(t0 - A_RADIUS * d, 0, seq - w), A_RADIUS)
        kw = k_ref[pl.ds(start, w), :]
        vw = v_ref[pl.ds(start, w), :]
        s = lax.dot_general(q_refs[g][...], kw, _NT, preferred_element_type=F32) * scale
        rel = rel_refs[g][...] + (start - t0).astype(F32)
        s = jnp.where(jnp.abs(rel) <= float(A_RADIUS * d), s, MASK_VALUE)
        m = jnp.max(s, -1, keepdims=True)
        p = jnp.exp(s - m)
        den = jnp.sum(p, -1, keepdims=True)
        outs.append(jnp.dot((p / den).astype(BF16), vw, preferred_element_type=F32))
        lses.append(m + jnp.log(den))
    mx = functools.reduce(jnp.maximum, lses)
    es = [jnp.exp(l - mx) for l in lses]
    tot = functools.reduce(lambda a, b: a + b, es)
    o_ref[...] = functools.reduce(lambda a, b: a + b, [(e / tot) * o for e, o in zip(es, outs)]).astype(o_ref.dtype)


def band_attention(qk, h, *, segs):
    n = qk.shape[0]
    dilations = tuple(d for _, d in A_PATTERNS)
    out = None
    for row0, batch, seq in segs:
        assert seq % A_QB == 0 and row0 % seq == 0
        nqb = seq // A_QB
        qb0, sb0 = row0 // A_QB, row0 // seq
        specs = [pl.BlockSpec((A_QB, LANES),
                              lambda b, hh, i, g=g, qb0=qb0, nqb=nqb: (qb0 + b * nqb + i, g * A_KV_HEADS + hh))
                 for g in range(len(dilations))]
        specs.append(pl.BlockSpec((seq, LANES), lambda b, hh, i, sb0=sb0: (sb0 + b, A_Q_HEADS + hh)))
        specs.append(pl.BlockSpec((seq, LANES), lambda b, hh, i, sb0=sb0: (sb0 + b, AV_OFF // LANES + hh)))
        args = [qk] * len(dilations) + [qk, h]
        for d in dilations:
            ji = np.arange(_band_window(seq, d))[None, :] - np.arange(A_QB)[:, None]
            rel = jnp.asarray(np.where(ji % d == 0, ji, OFF_PHASE), F32)
            specs.append(_resident(rel.shape))
            args.append(rel)
        aliases = _alias_previous(specs, args, out)
        out = pl.pallas_call(
            functools.partial(_band_body, seq=seq, dilations=dilations, scale=HEAD_DIM ** -0.5,
                              aliased=out is not None),
            grid=(batch, A_KV_HEADS, nqb), in_specs=specs,
            out_specs=pl.BlockSpec((A_QB, LANES), lambda b, hh, i, qb0=qb0, nqb=nqb: (qb0 + b * nqb + i, hh)),
            out_shape=jax.ShapeDtypeStruct((n, A_KV_HEADS * LANES), BF16),
            input_output_aliases=aliases,
            compiler_params=_params(("parallel", "arbitrary", "arbitrary")), name="band_attention")(*args)
    return out


def _sigmoid(x):
    return 1.0 / (1.0 + jnp.exp(-x))


def _layer_norm(y, g, b):
    mu = jnp.mean(y, -1, keepdims=True)
    yc = y - mu
    var = jnp.mean(yc * yc, -1, keepdims=True)
    return yc * lax.rsqrt(var + LN_EPS) * g + b


def _merge_body(oa_ref, ob_ref, oc_ref, g0_ref, g1_ref, g2_ref, wa_ref, wb_ref, wc_ref, o_ref):
    ya = jnp.dot(oa_ref[...], wa_ref[...], preferred_element_type=F32)
    yb = jnp.dot(ob_ref[...], wb_ref[...], preferred_element_type=F32)
    yc = jnp.dot(oc_ref[...], wc_ref[...], preferred_element_type=F32)
    merged = (_sigmoid(g0_ref[...].astype(F32)) * ya + _sigmoid(g1_ref[...].astype(F32)) * yb
              + _sigmoid(g2_ref[...].astype(F32)) * yc)
    o_ref[...] = merged.astype(o_ref.dtype)


def merge_mixers(oa, ob, oc, h, w_o_a, w_o_b, w_o_c, *, tm=256):
    n = ob.shape[0]
    tm = _tile(n, tm)
    row = lambda w: pl.BlockSpec((tm, w), lambda i: (i, 0))
    gate = lambda j: pl.BlockSpec((tm, D_MODEL), lambda i: (i, GATE_OFF // D_MODEL + j))
    specs = ([row(oa.shape[1]), row(ob.shape[1]), row(oc.shape[1]), gate(0), gate(1), gate(2)]
             + [_resident(w_o_a.shape), _resident(w_o_b.shape), _resident(w_o_c.shape)])
    return pl.pallas_call(
        _merge_body, grid=(n // tm,), in_specs=specs, out_specs=row(D_MODEL),
        out_shape=jax.ShapeDtypeStruct((n, D_MODEL), BF16),
        compiler_params=_params(("parallel",)), name="merge_mixers")(oa, ob, oc, h, h, h, w_o_a, w_o_b, w_o_c)


def _proj_ln_body(a_ref, w_ref, x_ref, g_ref, b_ref, o_ref, ob_ref, *, alpha):
    y = alpha * x_ref[...] + jnp.dot(a_ref[...], w_ref[...], preferred_element_type=F32)
    out = _layer_norm(y, g_ref[...], b_ref[...])
    o_ref[...] = out
    ob_ref[...] = out.astype(BF16)


def proj_residual_ln(a, w, x, g, b, *, alpha, tm=256):
    n, d = x.shape
    tm = _tile(n, tm)
    row = lambda wd: pl.BlockSpec((tm, wd), lambda i: (i, 0))
    vec = pl.BlockSpec((1, d), lambda i: (0, 0))
    return pl.pallas_call(
        functools.partial(_proj_ln_body, alpha=alpha), grid=(n // tm,),
        in_specs=[row(a.shape[1]), _resident(w.shape), row(d), vec, vec],
        out_specs=[row(d), row(d)],
        out_shape=[jax.ShapeDtypeStruct((n, d), F32), jax.ShapeDtypeStruct((n, d), BF16)],
        compiler_params=_params(("parallel",)), name="proj_residual_ln")(a, w, x, g.reshape(1, d), b.reshape(1, d))


def _split_bf16(x):
    hi = x.astype(BF16)
    lo = (x - hi.astype(F32)).astype(BF16)
    return hi, lo


def _router_body(x_ref, wh_ref, wl_ref, bias_ref, idx_ref, wt_ref):
    xh, xl = _split_bf16(x_ref[...])
    nt = (((1,), (1,)), ((), ()))
    wh, wl = wh_ref[...], wl_ref[...]
    logits = (lax.dot_general(wh, xh, nt, preferred_element_type=F32)
              + (lax.dot_general(wh, xl, nt, preferred_element_type=F32)
                 + lax.dot_general(wl, xh, nt, preferred_element_type=F32)))
    scores = _sigmoid(logits)
    biased = scores + bias_ref[...]
    tn = scores.shape[1]
    gsz = N_EXPERTS // N_EXPERT_GROUPS
    neg = -jnp.inf
    sub = lax.broadcasted_iota(jnp.int32, (gsz, tn), 0).astype(F32)
    gs = []
    for g in range(N_EXPERT_GROUPS):
        blk = biased[g * gsz:(g + 1) * gsz]
        m1 = jnp.max(blk, 0, keepdims=True)
        first = jnp.min(jnp.where(blk == m1, sub, float(gsz)), 0, keepdims=True)
        m2 = jnp.max(jnp.where(sub == first, neg, blk), 0, keepdims=True)
        gs.append(m1 + m2)
    gs = jnp.concatenate(gs, axis=0)
    gi = lax.broadcasted_iota(jnp.int32, (N_EXPERT_GROUPS, tn), 0).astype(F32)
    gsel = jnp.zeros((N_EXPERT_GROUPS, tn), F32)
    for _ in range(TOPK_GROUPS):
        gm = jnp.max(gs, 0, keepdims=True)
        pick = gi == jnp.min(jnp.where(gs == gm, gi, float(N_EXPERT_GROUPS)), 0, keepdims=True)
        gsel = jnp.where(pick, 1.0, gsel)
        gs = jnp.where(pick, neg, gs)
    masked = jnp.concatenate(
        [jnp.where(gsel[g:g + 1] > 0.0, biased[g * gsz:(g + 1) * gsz], neg) for g in range(N_EXPERT_GROUPS)], axis=0)
    ei = lax.broadcasted_iota(jnp.int32, (N_EXPERTS, tn), 0).astype(F32)
    idxs, wts = [], []
    for _ in range(TOP_K):
        mx = jnp.max(masked, 0, keepdims=True)
        ix = jnp.min(jnp.where(masked == mx, ei, float(N_EXPERTS)), 0, keepdims=True)
        pick = ei == ix
        idxs.append(ix)
        wts.append(jnp.sum(jnp.where(pick, scores, 0.0), 0, keepdims=True))
        masked = jnp.where(pick, neg, masked)
    w = jnp.concatenate(wts, axis=0)
    idx_ref[...] = jnp.concatenate(idxs, axis=0).astype(jnp.int32)
    wt_ref[...] = w / jnp.sum(w, 0, keepdims=True) * ROUTED_SCALE


def router(x, w_router, router_bias, *, tn=512):
    n, d = x.shape
    tn = _tile(n, tn)
    wt = w_router.T.astype(F32)
    wh = wt.astype(BF16)
    wl = (wt - wh.astype(F32)).astype(BF16)
    full = lambda r, c: pl.BlockSpec((r, c), lambda i: (0, 0))
    out = pl.BlockSpec((TOP_K, tn), lambda i: (0, i))
    return pl.pallas_call(
        _router_body, grid=(n // tn,),
        in_specs=[pl.BlockSpec((tn, d), lambda i: (i, 0)), full(N_EXPERTS, d), full(N_EXPERTS, d), full(N_EXPERTS, 1)],
        out_specs=[out, out],
        out_shape=[jax.ShapeDtypeStruct((TOP_K, n), jnp.int32), jax.ShapeDtypeStruct((TOP_K, n), F32)],
        compiler_params=_params(("parallel",)), name="router")(x, wh, wl, router_bias.reshape(N_EXPERTS, 1).astype(F32))


def _silu(x):
    return x * _sigmoid(x)


def _experts_body(blk_e_ref, nused_ref, x_ref, wg_ref, wu_ref, wd_ref, o_ref, wgu_scr, wd_scr):
    i = pl.program_id(0)
    used = i < nused_ref[0]
    new_expert = blk_e_ref[i] != blk_e_ref[jnp.maximum(i - 1, 0)]

    @pl.when(used & ((i == 0) | new_expert))
    def _():
        wgu_scr[:, :EXPERT_FF] = wg_ref[...].astype(BF16)
        wgu_scr[:, EXPERT_FF:] = wu_ref[...].astype(BF16)
        wd_scr[...] = wd_ref[...].astype(BF16)

    @pl.when(used)
    def _():
        hgu = jnp.dot(x_ref[...], wgu_scr[...], preferred_element_type=F32)
        a = _silu(hgu[:, :EXPERT_FF]) * hgu[:, EXPERT_FF:]
        o_ref[...] = jnp.dot(a.astype(BF16), wd_scr[...], preferred_element_type=F32).astype(o_ref.dtype)

    @pl.when(jnp.logical_not(used))
    def _():
        o_ref[...] = jnp.zeros(o_ref.shape, o_ref.dtype)


def grouped_experts(xs, blk_e, nused, w_gate, w_up, w_down, layer):
    p, d = xs.shape
    nb = p // MOE_ROWS
    grid_spec = pltpu.PrefetchScalarGridSpec(
        num_scalar_prefetch=2, grid=(nb,),
        in_specs=[pl.BlockSpec((MOE_ROWS, d), lambda i, e, u: (i, 0)),
                  pl.BlockSpec((None, None, d, EXPERT_FF), lambda i, e, u: (layer, e[i], 0, 0)),
                  pl.BlockSpec((None, None, d, EXPERT_FF), lambda i, e, u: (layer, e[i], 0, 0)),
                  pl.BlockSpec((None, None, EXPERT_FF, d), lambda i, e, u: (layer, e[i], 0, 0))],
        out_specs=pl.BlockSpec((MOE_ROWS, d), lambda i, e, u: (i, 0)),
        scratch_shapes=[pltpu.VMEM((d, 2 * EXPERT_FF), BF16), pltpu.VMEM((EXPERT_FF, d), BF16)])
    return pl.pallas_call(
        _experts_body, grid_spec=grid_spec, out_shape=jax.ShapeDtypeStruct((p, d), BF16),
        compiler_params=_params(("arbitrary",)), name="grouped_experts")(blk_e, nused, xs, w_gate, w_up, w_down)


def _shared_ln_body(x_ref, xb_ref, yg_ref, gw_ref, wgu_ref, wd_ref, g_ref, b_ref, o_ref, *ob_ref, alpha):
    hgu = jnp.dot(xb_ref[...], wgu_ref[...], preferred_element_type=F32)
    a = _silu(hgu[:, :SHARED_FF]) * hgu[:, SHARED_FF:]
    y = jnp.dot(a.astype(BF16), wd_ref[...], preferred_element_type=F32)
    gw = gw_ref[...]
    for k in range(TOP_K):
        y = y + yg_ref[k].astype(F32) * gw[:, k:k + 1]
    out = _layer_norm(alpha * x_ref[...] + y, g_ref[...], b_ref[...])
    o_ref[...] = out
    for ref in ob_ref:
        ref[...] = out.astype(BF16)


def shared_expert_ln(x, xb, yg, gate_w, w_gu, w_d, g, b, *, alpha, rows=None, with_bf16=True, tm=256):
    d = x.shape[1]
    row0, n = (0, x.shape[0]) if rows is None else rows
    tm = _tile(n, tm)
    assert row0 % tm == 0
    src = pl.BlockSpec((tm, d), lambda i: (row0 // tm + i, 0))
    yg_spec = pl.BlockSpec((TOP_K, tm, d), lambda i: (0, row0 // tm + i, 0))
    gw_spec = pl.BlockSpec((tm, TOP_K), lambda i: (row0 // tm + i, 0))
    dst = pl.BlockSpec((tm, d), lambda i: (i, 0))
    vec = pl.BlockSpec((1, d), lambda i: (0, 0))
    out_shape = [jax.ShapeDtypeStruct((n, d), F32)] + [jax.ShapeDtypeStruct((n, d), BF16)] * with_bf16
    return pl.pallas_call(
        functools.partial(_shared_ln_body, alpha=alpha), grid=(n // tm,),
        in_specs=[src, src, yg_spec, gw_spec, _resident(w_gu.shape), _resident(w_d.shape), vec, vec],
        out_specs=[dst] * len(out_shape), out_shape=out_shape,
        compiler_params=_params(("parallel",)), name="shared_expert_ln")(
            x, xb, yg, gate_w, w_gu, w_d, g.reshape(1, d), b.reshape(1, d))


def _dispatch_plan(idx, n):
    a = n * TOP_K
    flat_e = idx.T.reshape(a)
    iota = jnp.arange(a, dtype=jnp.int32)
    _, order = lax.sort_key_val(flat_e, iota)
    _, rank = lax.sort_key_val(order, iota)
    experts = jnp.arange(N_EXPERTS, dtype=jnp.int32)
    counts = jnp.sum((flat_e[:, None] == experts[None, :]).astype(jnp.int32), axis=0)
    padded = (counts + MOE_ROWS - 1) // MOE_ROWS * MOE_ROWS
    pad_end = jnp.cumsum(padded)
    pad_start = pad_end - padded
    start = jnp.cumsum(counts) - counts
    n_blocks = a // MOE_ROWS + N_EXPERTS
    blk0 = jnp.arange(n_blocks, dtype=jnp.int32) * MOE_ROWS
    blk_e = jnp.minimum(jnp.sum((pad_end[None, :] <= blk0[:, None]).astype(jnp.int32), axis=1), N_EXPERTS - 1)
    src = (start - pad_start)[blk_e][:, None] + blk0[:, None] + jnp.arange(MOE_ROWS, dtype=jnp.int32)[None, :]
    slot_token = order[jnp.clip(src, 0, a - 1).reshape(-1)] // TOP_K
    nused = (pad_end[-1] // MOE_ROWS).astype(jnp.int32).reshape(1)
    slot_of = ((pad_start - start)[flat_e] + rank).reshape(n, TOP_K)
    return slot_token, blk_e.astype(jnp.int32), nused, slot_of


def _prep_layer(l, p):
    bf = lambda a: a.astype(BF16)
    w_in = p['w_in'][l]
    offs = np.cumsum((0, A_Q_HEADS * HEAD_DIM, A_KV_HEADS * HEAD_DIM, A_KV_HEADS * HEAD_DIM, B_Q_RANK, B_KV_RANK,
                      B_ROPE, C_Q_HEADS * HEAD_DIM, C_KV_HEADS * HEAD_DIM, C_KV_HEADS * HEAD_DIM, N_BRANCH * D_MODEL))
    piece = lambda i: w_in[:, offs[i]:offs[i + 1]]
    zeros = lambda c: jnp.zeros((D_MODEL, c), w_in.dtype)
    w_h = jnp.concatenate([piece(9), piece(0), piece(1), piece(2), piece(3), piece(4), piece(5), zeros(LANES - B_ROPE),
                           piece(6), piece(7), piece(8), zeros(H_COLS - H_USED)], axis=1)
    w_uq = p['w_uq'][l].reshape(B_Q_RANK, B_HEADS, B_NOPE + B_ROPE)
    w_uq_rope = jnp.pad(w_uq[:, :, B_NOPE:], ((0, 0), (0, 0), (0, LANES - B_ROPE)))
    w_q = jnp.concatenate([w_uq[:, :, :B_NOPE].reshape(B_Q_RANK, -1), w_uq_rope.reshape(B_Q_RANK, -1)], axis=1)
    w_ukv = p['w_ukv'][l].reshape(B_KV_RANK, B_HEADS, B_NOPE + B_V)
    w_k = w_ukv[:, :, :B_NOPE].reshape(B_KV_RANK, -1)
    w_v_t = w_ukv[:, :, B_NOPE:].reshape(B_KV_RANK, -1).T
    return dict(
        w_h=bf(w_h), w_q=bf(w_q), w_k=bf(w_k), w_v_t=bf(w_v_t), w_cv_t=bf(piece(8).T),
        w_o_a=bf(p['w_o_a'][l]), w_o_b=bf(p['w_o_b'][l]), w_o_c=bf(p['w_o_c'][l]), w_out=bf(p['w_out'][l]),
        w_s_gu=bf(jnp.concatenate([p['w_s_gate'][l], p['w_s_up'][l]], axis=-1)), w_s_d=bf(p['w_s_down'][l]),
        c_gains=jnp.concatenate([jnp.tile(p['c_q_norm'][l][None] * (HEAD_DIM ** -0.5 * LOG2E), (C_Q_HEADS, 1)),
                                 jnp.tile(p['c_k_norm'][l][None], (C_KV_HEADS, 1))], axis=0),
    )


def _token_mixer(x, xb, lw, l, p, tabs, segs, alpha):
    tab_a, tab_b, tab_c = tabs
    h = matmul(xb, lw['w_h'], name="in_proj")
    assert all((window // dilation) // 2 == A_RADIUS for window, dilation in A_PATTERNS)
    qk_a = rope(h, tab_a, name="rope_a", segs=segs, col=AQ_OFF, heads=A_Q_HEADS + A_KV_HEADS, shift=PARTIAL_ROT // 2)
    o_a = band_attention(qk_a, h, segs=segs)
    q_b = matmul(h, lw['w_q'], name="mla_q", x_col=BCQ_OFF, gain=p['mla_q_norm'][l],
                 out_scale=(B_NOPE + B_ROPE) ** -0.5 * LOG2E)
    k_b = matmul(h, lw['w_k'], name="mla_k", x_col=BCKV_OFF, gain=p['mla_kv_norm'][l])
    vt_b = matmul_t(h, lw['w_v_t'], name="mla_v_t", tm=FLASH_KEYS, x_col=BCKV_OFF, gain=p['mla_kv_norm'][l])
    q_b_rope = rope(q_b, tab_b, name="rope_bq", segs=segs, col=B_HEADS * LANES, heads=B_HEADS, shift=B_ROPE // 2)
    k_b_rope = rope(h, tab_b, name="rope_bk", segs=segs, col=BKR_OFF, heads=1, shift=B_ROPE // 2)
    o_b = flash_attention([(q_b, 0, LANES), (q_b_rope, 0, LANES)], [(k_b, 0, LANES), (k_b_rope, 0, 0)], vt_b,
                          segs=segs, kv_heads=B_HEADS, group=1, name="flash_b")
    qk_c = rope(h, tab_c, name="rope_c", segs=segs, col=CQ_OFF, heads=C_Q_HEADS + C_KV_HEADS, shift=HEAD_DIM // 4,
                gains=lw['c_gains'])
    vt_c = matmul_t(xb, lw['w_cv_t'], name="c_v_t", tm=FLASH_KEYS)
    group = C_Q_HEADS // C_KV_HEADS
    o_c = flash_attention([(qk_c, 0, group * LANES)], [(qk_c, C_Q_HEADS * LANES, LANES)], vt_c,
                          segs=segs, kv_heads=C_KV_HEADS, group=group, name="flash_c")
    merged = merge_mixers(o_a, o_b, o_c, h, lw['w_o_a'], lw['w_o_b'], lw['w_o_c'])
    return proj_residual_ln(merged, lw['w_out'], x, p['ln1_g'][l], p['ln1_b'][l], alpha=alpha)


def _moe(x, xb, lw, l, p, alpha, out_rows=None):
    n = x.shape[0]
    idx, wts = router(x, p['w_router'][l], p['router_bias'][l])
    slot_token, blk_e, nused, slot_of = _dispatch_plan(idx, n)
    xs = xb.at[slot_token].get(mode='promise_in_bounds')
    y = grouped_experts(xs, blk_e, nused, p['w_e_gate'], p['w_e_up'], p['w_e_down'], l)
    yg = y.at[slot_of.T].get(mode='promise_in_bounds')
    final = functools.partial(shared_expert_ln, x, xb, yg, wts.T, lw['w_s_gu'], lw['w_s_d'], p['ln2_g'][l],
                              p['ln2_b'][l], alpha=alpha)
    if out_rows is None:
        return final()
    return tuple(final(rows=rows, with_bf16=False)[0] for rows in out_rows)


def _run_trunk(xs3, layers, p):
    d = xs3[0].shape[-1]
    segs, row0 = [], 0
    for x3 in xs3:
        segs.append((row0, x3.shape[0], x3.shape[1]))
        row0 += x3.shape[0] * x3.shape[1]
    segs = tuple(segs)
    alpha = (2 * len(layers)) ** 0.25
    tabs = _rope_tables(max(seq for _, _, seq in segs))
    x = jnp.concatenate([x3.reshape(-1, d) for x3 in xs3], axis=0)
    xb = x.astype(BF16)
    last = len(layers) - 1
    for l, lw in enumerate(layers):
        x, xb = _token_mixer(x, xb, lw, l, p, tabs, segs, alpha)
        if l < last:
            x, xb = _moe(x, xb, lw, l, p, alpha)
    outs = _moe(x, xb, layers[last], last, p, alpha, out_rows=tuple((r0, b * s) for r0, b, s in segs))
    return tuple(o.reshape(b, s, d) for o, (_, b, s) in zip(outs, segs))


def kernel(x_prompt, x_sample, ln1_g, ln1_b, w_in, mla_q_norm, w_uq, mla_kv_norm, w_ukv, c_q_norm, c_k_norm, w_o_a, w_o_b, w_o_c, w_out, ln2_g, ln2_b, w_router, router_bias, w_e_gate, w_e_up, w_e_down, w_s_gate, w_s_up, w_s_down):
    p = dict(ln1_g=ln1_g, ln1_b=ln1_b, w_in=w_in, mla_q_norm=mla_q_norm, w_uq=w_uq, mla_kv_norm=mla_kv_norm,
             w_ukv=w_ukv, c_q_norm=c_q_norm, c_k_norm=c_k_norm, w_o_a=w_o_a, w_o_b=w_o_b, w_o_c=w_o_c, w_out=w_out,
             ln2_g=ln2_g, ln2_b=ln2_b, w_router=w_router, router_bias=router_bias, w_e_gate=w_e_gate,
             w_e_up=w_e_up, w_e_down=w_e_down, w_s_gate=w_s_gate, w_s_up=w_s_up, w_s_down=w_s_down)
    layers = [_prep_layer(l, p) for l in range(w_in.shape[0])]
    return _run_trunk((x_prompt, x_sample), layers, p)
```

```python
import functools

import numpy as np
import jax
import jax.numpy as jnp
from jax import lax
from jax.experimental import pallas as pl
from jax.experimental.pallas import tpu as pltpu

F32 = jnp.float32
BF16 = jnp.bfloat16

D_MODEL = 2048
HEAD_DIM = 128
A_PATTERNS = ((128, 1), (512, 4), (2048, 16))
A_KV_HEADS = 4
A_Q_HEADS = len(A_PATTERNS) * A_KV_HEADS
ROPE_THETA = 500000.0
PARTIAL_ROT = HEAD_DIM // 4
B_HEADS = 8
B_Q_RANK = 512
B_KV_RANK = 256
B_NOPE = 128
B_ROPE = 64
B_V = 128
C_Q_HEADS = 8
C_KV_HEADS = 2
AXIAL_THETA = 10000.0
GRID_W = 64
N_BRANCH = 3
N_EXPERTS = 64
TOP_K = 8
N_EXPERT_GROUPS = 8
TOPK_GROUPS = 4
EXPERT_FF = 512
SHARED_FF = 512
ROUTED_SCALE = 2.5
LN_EPS = 1e-5
RMS_EPS = 1e-6
MASK_VALUE = -1e30

LANES = 128
VMEM_LIMIT = 56 * 1024 * 1024
_NT = (((1,), (1,)), ((), ()))

GATE_OFF = 0
AQ_OFF = GATE_OFF + N_BRANCH * D_MODEL
AK_OFF = AQ_OFF + A_Q_HEADS * HEAD_DIM
AV_OFF = AK_OFF + A_KV_HEADS * HEAD_DIM
BCQ_OFF = AV_OFF + A_KV_HEADS * HEAD_DIM
BCKV_OFF = BCQ_OFF + B_Q_RANK
BKR_OFF = BCKV_OFF + B_KV_RANK
CQ_OFF = BKR_OFF + LANES
CK_OFF = CQ_OFF + C_Q_HEADS * HEAD_DIM
CV_OFF = CK_OFF + C_KV_HEADS * HEAD_DIM
H_USED = CV_OFF + C_KV_HEADS * HEAD_DIM
H_COLS = -(-H_USED // 1024) * 1024

MOE_ROWS = 256


def _tile(n, t):
    t = min(n, t)
    assert n % t == 0, (n, t)
    return t


def _params(sem):
    return pltpu.CompilerParams(dimension_semantics=sem, vmem_limit_bytes=VMEM_LIMIT)


def _resident(shape):
    nd = len(shape)
    return pl.BlockSpec(shape, lambda *_: (0,) * nd, pipeline_mode=pl.Buffered(1))


def _mm_body(x_ref, w_ref, o_ref):
    o_ref[...] = jnp.dot(x_ref[...], w_ref[...], preferred_element_type=F32).astype(o_ref.dtype)


def _mm_rms_body(x_ref, g_ref, w_ref, o_ref, *, out_scale):
    xf = x_ref[...].astype(F32)
    y = xf * lax.rsqrt(jnp.mean(xf * xf, -1, keepdims=True) + RMS_EPS) * g_ref[...]
    acc = jnp.dot(y.astype(BF16), w_ref[...], preferred_element_type=F32)
    o_ref[...] = (acc * out_scale).astype(o_ref.dtype)


def matmul(x, w, *, name, x_col=0, gain=None, out_scale=1.0, tm=1024, tn=1024, out_dtype=BF16):
    m = x.shape[0]
    k, n = w.shape
    tm, tn = _tile(m, tm), _tile(n, tn)
    assert x_col % k == 0
    x_spec = pl.BlockSpec((tm, k), lambda i, j: (i, x_col // k))
    w_spec = pl.BlockSpec((k, tn), lambda i, j: (0, j))
    o_spec = pl.BlockSpec((tm, tn), lambda i, j: (i, j))
    if gain is None:
        assert out_scale == 1.0
        body, specs, args = _mm_body, [x_spec, w_spec], (x, w)
    else:
        g_spec = pl.BlockSpec((1, k), lambda i, j: (0, 0))
        body = functools.partial(_mm_rms_body, out_scale=out_scale)
        specs, args = [x_spec, g_spec, w_spec], (x, gain.reshape(1, k).astype(F32), w)
    return pl.pallas_call(
        body, grid=(m // tm, n // tn), in_specs=specs, out_specs=o_spec,
        out_shape=jax.ShapeDtypeStruct((m, n), out_dtype),
        compiler_params=_params(("parallel", "arbitrary")), name=name)(*args)


def _mm_t_body(x_ref, w_ref, o_ref):
    o_ref[...] = lax.dot_general(w_ref[...], x_ref[...], _NT, preferred_element_type=F32).astype(o_ref.dtype)


def _mm_t_rms_body(x_ref, g_ref, w_ref, o_ref):
    xf = x_ref[...].astype(F32)
    y = xf * lax.rsqrt(jnp.mean(xf * xf, -1, keepdims=True) + RMS_EPS) * g_ref[...]
    o_ref[...] = lax.dot_general(w_ref[...], y.astype(BF16), _NT, preferred_element_type=F32).astype(o_ref.dtype)


def matmul_t(x, w_t, *, name, tm, x_col=0, gain=None):
    m = x.shape[0]
    n, k = w_t.shape
    assert m % tm == 0 and x_col % k == 0
    x_spec = pl.BlockSpec((tm, k), lambda i: (i, x_col // k))
    o_spec = pl.BlockSpec((None, n, tm), lambda i: (i, 0, 0))
    if gain is None:
        body, specs, args = _mm_t_body, [x_spec, _resident(w_t.shape)], (x, w_t)
    else:
        g_spec = pl.BlockSpec((1, k), lambda i: (0, 0))
        body, specs, args = _mm_t_rms_body, [x_spec, g_spec, _resident(w_t.shape)], (x, gain.reshape(1, k).astype(F32), w_t)
    return pl.pallas_call(
        body, grid=(m // tm,), in_specs=specs, out_specs=o_spec,
        out_shape=jax.ShapeDtypeStruct((m // tm, n, tm), BF16),
        compiler_params=_params(("parallel",)), name=name)(*args)


def _rope_body(*refs, shift, rms, heads):
    if rms:
        x_ref, c_ref, s1_ref, s2_ref, g_ref, o_ref = refs
    else:
        x_ref, c_ref, s1_ref, s2_ref, o_ref = refs
    c, s1, s2 = c_ref[...], s1_ref[...], s2_ref[...]
    for hd in range(heads):
        sl = slice(hd * LANES, (hd + 1) * LANES)
        x = x_ref[:, sl].astype(F32)
        if rms:
            x = x * lax.rsqrt(jnp.mean(x * x, -1, keepdims=True) + RMS_EPS) * g_ref[hd]
        out = x * c + pltpu.roll(x, LANES - shift, 1) * s1 + pltpu.roll(x, shift, 1) * s2
        o_ref[:, sl] = out.astype(o_ref.dtype)


def rope(x, tabs, *, name, segs, col, heads, shift, gains=None, ts=1024):
    n = x.shape[0]
    ts = min([ts] + [seq for _, _, seq in segs])
    assert n % ts == 0 and all(row0 % ts == 0 and seq % ts == 0 for row0, _, seq in segs)
    blk = col // LANES

    def tab_block(r):
        idx = r * 0
        for row0, _, seq in segs:
            idx = jnp.where(r >= row0 // ts, (r - row0 // ts) % (seq // ts), idx)
        return idx

    hp = max(c for c in range(1, 9) if heads % c == 0 and blk % c == 0)
    x_spec = pl.BlockSpec((ts, hp * LANES), lambda r, h: (r, blk // hp + h))
    t_spec = pl.BlockSpec((ts, LANES), lambda r, h: (tab_block(r), 0))
    o_spec = pl.BlockSpec((ts, hp * LANES), lambda r, h: (r, h))
    specs, args = [x_spec, t_spec, t_spec, t_spec], [x, *tabs]
    if gains is not None:
        specs.append(pl.BlockSpec((hp, 1, LANES), lambda r, h: (h, 0, 0)))
        args.append(gains.reshape(heads, 1, LANES).astype(F32))
    return pl.pallas_call(
        functools.partial(_rope_body, shift=shift, rms=gains is not None, heads=hp),
        grid=(n // ts, heads // hp), in_specs=specs, out_specs=o_spec,
        out_shape=jax.ShapeDtypeStruct((n, heads * LANES), BF16),
        compiler_params=_params(("parallel", "arbitrary")), name=name)(*args)


def _rope_tables(seq):
    def table(pos, dim, theta):
        inv = theta ** (-jnp.arange(0, dim, 2, dtype=F32) / dim)
        ang = pos.astype(F32)[:, None] * inv[None, :]
        return jnp.cos(ang), jnp.sin(ang)

    pos = jnp.arange(seq)
    row = pos // GRID_W
    colp = pos % GRID_W
    one = lambda n: jnp.ones((seq, n), F32)
    zero = lambda n: jnp.zeros((seq, n), F32)
    cat = lambda *xs: jnp.concatenate(xs, -1)
    ca, sa = table(pos, PARTIAL_ROT, ROPE_THETA)
    ra = LANES - PARTIAL_ROT
    tab_a = (cat(ca, ca, one(ra)), cat(-sa, zero(LANES - 16)), cat(zero(16), sa, zero(ra)))
    cb, sb = table(pos, B_ROPE, ROPE_THETA)
    tab_b = (cat(cb, cb, one(64)), cat(-sb, zero(96)), cat(zero(32), sb, zero(64)))
    cr, sr = table(row, HEAD_DIM // 2, AXIAL_THETA)
    cc, sc = table(colp, HEAD_DIM // 2, AXIAL_THETA)
    tab_c = (cat(cr, cr, cc, cc), cat(-sr, zero(32), -sc, zero(32)), cat(zero(32), sr, zero(32), sc))
    return tab_a, tab_b, tab_c


FLASH_KEYS = 512
LOG2E = 1.4426950408889634


FLASH_STRIP = 512
ONES_ROWS = 16


def _flash_body(*refs, nq, nk, group, tq, nchunks, aliased):
    q_refs, k_refs, vt_ref = refs[:nq], refs[nq:nq + nk], refs[nq + nk]
    o_ref, qt_scr, m_scr, mblk_scr, acc_scr, s_even, s_odd = refs[nq + nk + 1 + aliased:]
    if group > 1:
        for g in range(group):
            qt_scr[:, g * tq:(g + 1) * tq] = q_refs[0][:, g * LANES:(g + 1) * LANES].astype(F32).T.astype(BF16)
    else:
        for i, r in enumerate(q_refs):
            qt_scr[i * LANES:(i + 1) * LANES, :] = r[...].astype(F32).T.astype(BF16)
    nq_all = group * tq
    ones = jnp.ones((ONES_ROWS, FLASH_KEYS), BF16)

    def keys(j):
        off = pl.multiple_of(j * FLASH_KEYS, FLASH_KEYS)
        return jnp.concatenate([r[pl.ds(off, FLASH_KEYS), :] for r in k_refs], axis=1)

    s0 = jnp.dot(keys(0), qt_scr[...], preferred_element_type=F32)
    s_even[...] = s0
    mblk_scr[...] = jnp.max(s0, 0, keepdims=True)
    m_scr[...] = jnp.full(m_scr.shape, -jnp.inf, F32)
    acc_scr[...] = jnp.zeros(acc_scr.shape, F32)

    def block(j, s_cur, s_nxt):
        k_next = keys(jnp.minimum(j + 1, nchunks - 1))
        vt1 = jnp.concatenate([vt_ref[j], ones], axis=0)
        m_prev = m_scr[...]
        m_new = jnp.maximum(m_prev, mblk_scr[...])
        alpha = jnp.exp2(m_prev - m_new)
        m_scr[...] = m_new
        for st in range(nq_all // FLASH_STRIP):
            cs = slice(st * FLASH_STRIP, (st + 1) * FLASH_STRIP)
            s_n = jnp.dot(k_next, qt_scr[:, cs], preferred_element_type=F32)
            s_nxt[:, cs] = s_n
            mblk_scr[:, cs] = jnp.max(s_n, 0, keepdims=True)
            p = jnp.exp2(s_cur[:, cs] - m_new[:, cs]).astype(BF16)
            acc_scr[:, cs] = alpha[:, cs] * acc_scr[:, cs] + jnp.dot(vt1, p, preferred_element_type=F32)

    def pair(i, carry):
        block(2 * i, s_even, s_odd)
        block(2 * i + 1, s_odd, s_even)
        return carry

    lax.fori_loop(0, nchunks // 2, pair, 0)
    acc = acc_scr[...]
    o_t = acc[:LANES] / acc[LANES:LANES + 1]
    for g in range(group):
        o_ref[:, g * LANES:(g + 1) * LANES] = o_t[:, g * tq:(g + 1) * tq].T.astype(o_ref.dtype)


def _alias_previous(specs, args, prev):
    if prev is None:
        return {}
    specs.append(pl.BlockSpec(memory_space=pl.ANY))
    args.append(prev)
    return {len(args) - 1: 0}


def flash_attention(q_srcs, k_srcs, v_t, *, segs, kv_heads, group, name, rows=1024):
    n = q_srcs[0][0].shape[0]
    nq, nk = len(q_srcs), len(k_srcs)
    qw = group * LANES
    dq = LANES if group > 1 else nq * LANES
    out = None
    for row0, batch, seq in segs:
        assert seq % (2 * FLASH_KEYS) == 0 and row0 % seq == 0
        tq = _tile(seq, rows // group)
        nqb, nchunks = seq // tq, seq // FLASH_KEYS
        r = group * tq
        assert r % FLASH_STRIP == 0
        qb0, sb0 = row0 // tq, row0 // seq
        specs, args = [], []
        for arr, col, stride in q_srcs:
            assert col % qw == 0 and stride % qw == 0
            specs.append(pl.BlockSpec((tq, qw), lambda b, h, i, c=col // qw, st=stride // qw, qb0=qb0, nqb=nqb:
                                      (qb0 + b * nqb + i, c + h * st)))
            args.append(arr)
        for arr, col, stride in k_srcs:
            specs.append(pl.BlockSpec((seq, LANES), lambda b, h, i, c=col // LANES, st=stride // LANES, sb0=sb0:
                                      (sb0 + b, c + h * st)))
            args.append(arr)
        specs.append(pl.BlockSpec((nchunks, LANES, FLASH_KEYS), lambda b, h, i, sb0=sb0: (sb0 + b, h, 0)))
        args.append(v_t)
        aliases = _alias_previous(specs, args, out)
        out = pl.pallas_call(
            functools.partial(_flash_body, nq=nq, nk=nk, group=group, tq=tq, nchunks=nchunks, aliased=out is not None),
            grid=(batch, kv_heads, nqb), in_specs=specs,
            out_specs=pl.BlockSpec((tq, qw), lambda b, h, i, qb0=qb0, nqb=nqb: (qb0 + b * nqb + i, h)),
            out_shape=jax.ShapeDtypeStruct((n, kv_heads * qw), BF16),
            scratch_shapes=[pltpu.VMEM((dq, r), BF16),
                            pltpu.VMEM((1, r), F32), pltpu.VMEM((1, r), F32),
                            pltpu.VMEM((LANES + ONES_ROWS, r), F32),
                            pltpu.VMEM((FLASH_KEYS, r), F32), pltpu.VMEM((FLASH_KEYS, r), F32)],
            input_output_aliases=aliases,
            compiler_params=_params(("parallel", "arbitrary", "arbitrary")), name=name)(*args)
    return out


A_QB = 256
A_RADIUS = 64


def _band_window(seq, dilation):
    return min(A_QB + 2 * A_RADIUS * dilation, seq)


OFF_PHASE = 1e9


def _band_body(*refs, seq, dilations, scale, aliased):
    ng = len(dilations)
    q_refs, k_ref, v_ref = refs[:ng], refs[ng], refs[ng + 1]
    rel_refs = refs[ng + 2:2 * ng + 2]
    o_ref = refs[2 * ng + 2 + aliased]
    t0 = pl.program_id(2) * A_QB
    outs, lses = [], []
    for g, d in enumerate(dilations):
        w = _band_window(seq, d)
        start = pl.multiple_of(jnp.clip(t0 - A_RADIUS * d, 0, seq - w), A_RADIUS)
        kw = k_ref[pl.ds(start, w), :]
        vw = v_ref[pl.ds(start, w), :]
        s = lax.dot_general(q_refs[g][...], kw, _NT, preferred_element_type=F32) * scale
        rel = rel_refs[g][...] + (start - t0).astype(F32)
        s = jnp.where(jnp.abs(rel) <= float(A_RADIUS * d), s, MASK_VALUE)
        m = jnp.max(s, -1, keepdims=True)
        p = jnp.exp(s - m)
        den = jnp.sum(p, -1, keepdims=True)
        outs.append(jnp.dot((p / den).astype(BF16), vw, preferred_element_type=F32))
        lses.append(m + jnp.log(den))
    mx = functools.reduce(jnp.maximum, lses)
    es = [jnp.exp(l - mx) for l in lses]
    tot = functools.reduce(lambda a, b: a + b, es)
    o_ref[...] = functools.reduce(lambda a, b: a + b, [(e / tot) * o for e, o in zip(es, outs)]).astype(o_ref.dtype)


def band_attention(qk, h, *, segs):
    n = qk.shape[0]
    dilations = tuple(d for _, d in A_PATTERNS)
    out = None
    for row0, batch, seq in segs:
        assert seq % A_QB == 0 and row0 % seq == 0
        nqb = seq // A_QB
        qb0, sb0 = row0 // A_QB, row0 // seq
        specs = [pl.BlockSpec((A_QB, LANES),
                              lambda b, hh, i, g=g, qb0=qb0, nqb=nqb: (qb0 + b * nqb + i, g * A_KV_HEADS + hh))
                 for g in range(len(dilations))]
        specs.append(pl.BlockSpec((seq, LANES), lambda b, hh, i, sb0=sb0: (sb0 + b, A_Q_HEADS + hh)))
        specs.append(pl.BlockSpec((seq, LANES), lambda b, hh, i, sb0=sb0: (sb0 + b, AV_OFF // LANES + hh)))
        args = [qk] * len(dilations) + [qk, h]
        for d in dilations:
            ji = np.arange(_band_window(seq, d))[None, :] - np.arange(A_QB)[:, None]
            rel = jnp.asarray(np.where(ji % d == 0, ji, OFF_PHASE), F32)
            specs.append(_resident(rel.shape))
            args.append(rel)
        aliases = _alias_previous(specs, args, out)
        out = pl.pallas_call(
            functools.partial(_band_body, seq=seq, dilations=dilations, scale=HEAD_DIM ** -0.5,
                              aliased=out is not None),
            grid=(batch, A_KV_HEADS, nqb), in_specs=specs,
            out_specs=pl.BlockSpec((A_QB, LANES), lambda b, hh, i, qb0=qb0, nqb=nqb: (qb0 + b * nqb + i, hh)),
            out_shape=jax.ShapeDtypeStruct((n, A_KV_HEADS * LANES), BF16),
            input_output_aliases=aliases,
            compiler_params=_params(("parallel", "arbitrary", "arbitrary")), name="band_attention")(*args)
    return out


def _sigmoid(x):
    return 1.0 / (1.0 + jnp.exp(-x))


def _layer_norm(y, g, b):
    mu = jnp.mean(y, -1, keepdims=True)
    yc = y - mu
    var = jnp.mean(yc * yc, -1, keepdims=True)
    return yc * lax.rsqrt(var + LN_EPS) * g + b


def _merge_body(oa_ref, ob_ref, oc_ref, g0_ref, g1_ref, g2_ref, wa_ref, wb_ref, wc_ref, o_ref):
    ya = jnp.dot(oa_ref[...], wa_ref[...], preferred_element_type=F32)
    yb = jnp.dot(ob_ref[...], wb_ref[...], preferred_element_type=F32)
    yc = jnp.dot(oc_ref[...], wc_ref[...], preferred_element_type=F32)
    merged = (_sigmoid(g0_ref[...].astype(F32)) * ya + _sigmoid(g1_ref[...].astype(F32)) * yb
              + _sigmoid(g2_ref[...].astype(F32)) * yc)
    o_ref[...] = merged.astype(o_ref.dtype)


def merge_mixers(oa, ob, oc, h, w_o_a, w_o_b, w_o_c, *, tm=256):
    n = ob.shape[0]
    tm = _tile(n, tm)
    row = lambda w: pl.BlockSpec((tm, w), lambda i: (i, 0))
    gate = lambda j: pl.BlockSpec((tm, D_MODEL), lambda i: (i, GATE_OFF // D_MODEL + j))
    specs = ([row(oa.shape[1]), row(ob.shape[1]), row(oc.shape[1]), gate(0), gate(1), gate(2)]
             + [_resident(w_o_a.shape), _resident(w_o_b.shape), _resident(w_o_c.shape)])
    return pl.pallas_call(
        _merge_body, grid=(n // tm,), in_specs=specs, out_specs=row(D_MODEL),
        out_shape=jax.ShapeDtypeStruct((n, D_MODEL), BF16),
        compiler_params=_params(("parallel",)), name="merge_mixers")(oa, ob, oc, h, h, h, w_o_a, w_o_b, w_o_c)


def _proj_ln_body(a_ref, w_ref, x_ref, g_ref, b_ref, o_ref, ob_ref, *, alpha):
    y = alpha * x_ref[...] + jnp.dot(a_ref[...], w_ref[...], preferred_element_type=F32)
    out = _layer_norm(y, g_ref[...], b_ref[...])
    o_ref[...] = out
    ob_ref[...] = out.astype(BF16)


def proj_residual_ln(a, w, x, g, b, *, alpha, tm=256):
    n, d = x.shape
    tm = _tile(n, tm)
    row = lambda wd: pl.BlockSpec((tm, wd), lambda i: (i, 0))
    vec = pl.BlockSpec((1, d), lambda i: (0, 0))
    return pl.pallas_call(
        functools.partial(_proj_ln_body, alpha=alpha), grid=(n // tm,),
        in_specs=[row(a.shape[1]), _resident(w.shape), row(d), vec, vec],
        out_specs=[row(d), row(d)],
        out_shape=[jax.ShapeDtypeStruct((n, d), F32), jax.ShapeDtypeStruct((n, d), BF16)],
        compiler_params=_params(("parallel",)), name="proj_residual_ln")(a, w, x, g.reshape(1, d), b.reshape(1, d))


def _split_bf16(x):
    hi = x.astype(BF16)
    lo = (x - hi.astype(F32)).astype(BF16)
    return hi, lo


def _router_body(x_ref, wh_ref, wl_ref, bias_ref, idx_ref, wt_ref):
    xh, xl = _split_bf16(x_ref[...])
    nt = (((1,), (1,)), ((), ()))
    wh, wl = wh_ref[...], wl_ref[...]
    logits = (lax.dot_general(wh, xh, nt, preferred_element_type=F32)
              + (lax.dot_general(wh, xl, nt, preferred_element_type=F32)
                 + lax.dot_general(wl, xh, nt, preferred_element_type=F32)))
    scores = _sigmoid(logits)
    biased = scores + bias_ref[...]
    tn = scores.shape[1]
    gsz = N_EXPERTS // N_EXPERT_GROUPS
    neg = -jnp.inf
    sub = lax.broadcasted_iota(jnp.int32, (gsz, tn), 0).astype(F32)
    gs = []
    for g in range(N_EXPERT_GROUPS):
        blk = biased[g * gsz:(g + 1) * gsz]
        m1 = jnp.max(blk, 0, keepdims=True)
        first = jnp.min(jnp.where(blk == m1, sub, float(gsz)), 0, keepdims=True)
        m2 = jnp.max(jnp.where(sub == first, neg, blk), 0, keepdims=True)
        gs.append(m1 + m2)
    gs = jnp.concatenate(gs, axis=0)
    gi = lax.broadcasted_iota(jnp.int32, (N_EXPERT_GROUPS, tn), 0).astype(F32)
    gsel = jnp.zeros((N_EXPERT_GROUPS, tn), F32)
    for _ in range(TOPK_GROUPS):
        gm = jnp.max(gs, 0, keepdims=True)
        pick = gi == jnp.min(jnp.where(gs == gm, gi, float(N_EXPERT_GROUPS)), 0, keepdims=True)
        gsel = jnp.where(pick, 1.0, gsel)
        gs = jnp.where(pick, neg, gs)
    masked = jnp.concatenate(
        [jnp.where(gsel[g:g + 1] > 0.0, biased[g * gsz:(g + 1) * gsz], neg) for g in range(N_EXPERT_GROUPS)], axis=0)
    ei = lax.broadcasted_iota(jnp.int32, (N_EXPERTS, tn), 0).astype(F32)
    idxs, wts = [], []
    for _ in range(TOP_K):
        mx = jnp.max(masked, 0, keepdims=True)
        ix = jnp.min(jnp.where(masked == mx, ei, float(N_EXPERTS)), 0, keepdims=True)
        pick = ei == ix
        idxs.append(ix)
        wts.append(jnp.sum(jnp.where(pick, scores, 0.0), 0, keepdims=True))
        masked = jnp.where(pick, neg, masked)
    w = jnp.concatenate(wts, axis=0)
    idx_ref[...] = jnp.concatenate(idxs, axis=0).astype(jnp.int32)
    wt_ref[...] = w / jnp.sum(w, 0, keepdims=True) * ROUTED_SCALE


def router(x, w_router, router_bias, *, tn=512):
    n, d = x.shape
    tn = _tile(n, tn)
    wt = w_router.T.astype(F32)
    wh = wt.astype(BF16)
    wl = (wt - wh.astype(F32)).astype(BF16)
    full = lambda r, c: pl.BlockSpec((r, c), lambda i: (0, 0))
    out = pl.BlockSpec((TOP_K, tn), lambda i: (0, i))
    return pl.pallas_call(
        _router_body, grid=(n // tn,),
        in_specs=[pl.BlockSpec((tn, d), lambda i: (i, 0)), full(N_EXPERTS, d), full(N_EXPERTS, d), full(N_EXPERTS, 1)],
        out_specs=[out, out],
        out_shape=[jax.ShapeDtypeStruct((TOP_K, n), jnp.int32), jax.ShapeDtypeStruct((TOP_K, n), F32)],
        compiler_params=_params(("parallel",)), name="router")(x, wh, wl, router_bias.reshape(N_EXPERTS, 1).astype(F32))


def _silu(x):
    return x * _sigmoid(x)


def _experts_body(blk_e_ref, nused_ref, x_ref, wg_ref, wu_ref, wd_ref, o_ref, wgu_scr, wd_scr):
    i = pl.program_id(0)
    used = i < nused_ref[0]
    new_expert = blk_e_ref[i] != blk_e_ref[jnp.maximum(i - 1, 0)]

    @pl.when(used & ((i == 0) | new_expert))
    def _():
        wgu_scr[:, :EXPERT_FF] = wg_ref[...].astype(BF16)
        wgu_scr[:, EXPERT_FF:] = wu_ref[...].astype(BF16)
        wd_scr[...] = wd_ref[...].astype(BF16)

    @pl.when(used)
    def _():
        hgu = jnp.dot(x_ref[...], wgu_scr[...], preferred_element_type=F32)
        a = _silu(hgu[:, :EXPERT_FF]) * hgu[:, EXPERT_FF:]
        o_ref[...] = jnp.dot(a.astype(BF16), wd_scr[...], preferred_element_type=F32).astype(o_ref.dtype)

    @pl.when(jnp.logical_not(used))
    def _():
        o_ref[...] = jnp.zeros(o_ref.shape, o_ref.dtype)


def grouped_experts(xs, blk_e, nused, w_gate, w_up, w_down, layer):
    p, d = xs.shape
    nb = p // MOE_ROWS
    grid_spec = pltpu.PrefetchScalarGridSpec(
        num_scalar_prefetch=2, grid=(nb,),
        in_specs=[pl.BlockSpec((MOE_ROWS, d), lambda i, e, u: (i, 0)),
                  pl.BlockSpec((None, None, d, EXPERT_FF), lambda i, e, u: (layer, e[i], 0, 0)),
                  pl.BlockSpec((None, None, d, EXPERT_FF), lambda i, e, u: (layer, e[i], 0, 0)),
                  pl.BlockSpec((None, None, EXPERT_FF, d), lambda i, e, u: (layer, e[i], 0, 0))],
        out_specs=pl.BlockSpec((MOE_ROWS, d), lambda i, e, u: (i, 0)),
        scratch_shapes=[pltpu.VMEM((d, 2 * EXPERT_FF), BF16), pltpu.VMEM((EXPERT_FF, d), BF16)])
    return pl.pallas_call(
        _experts_body, grid_spec=grid_spec, out_shape=jax.ShapeDtypeStruct((p, d), BF16),
        compiler_params=_params(("arbitrary",)), name="grouped_experts")(blk_e, nused, xs, w_gate, w_up, w_down)


def _shared_ln_body(*refs, alpha, nout):
    x_ref, xb_ref, yg_ref, gw_ref, wgu_ref, wd_ref, g_ref, b_ref = refs[:8]
    outs = refs[len(refs) - nout:]
    o_ref, ob_ref = outs[0], outs[1:]
    hgu = jnp.dot(xb_ref[...], wgu_ref[...], preferred_element_type=F32)
    a = _silu(hgu[:, :SHARED_FF]) * hgu[:, SHARED_FF:]
    y = jnp.dot(a.astype(BF16), wd_ref[...], preferred_element_type=F32)
    gw = gw_ref[...]
    for k in range(TOP_K):
        y = y + yg_ref[k].astype(F32) * gw[:, k:k + 1]
    out = _layer_norm(alpha * x_ref[...] + y, g_ref[...], b_ref[...])
    o_ref[...] = out
    for ref in ob_ref:
        ref[...] = out.astype(BF16)


def shared_expert_ln(x, xb, yg, gate_w, w_gu, w_d, g, b, *, alpha, row0, out_row0, out_rows, prev, tm=256):
    d = x.shape[1]
    n = yg.shape[1]
    tm = _tile(n, tm)
    assert row0 % tm == 0 and out_row0 % tm == 0
    src = pl.BlockSpec((tm, d), lambda i: (row0 // tm + i, 0))
    yg_spec = pl.BlockSpec((TOP_K, tm, d), lambda i: (0, i, 0))
    gw_spec = pl.BlockSpec((tm, TOP_K), lambda i: (i, 0))
    dst = pl.BlockSpec((tm, d), lambda i: (out_row0 // tm + i, 0))
    vec = pl.BlockSpec((1, d), lambda i: (0, 0))
    out_shape = [jax.ShapeDtypeStruct((out_rows, d), dt) for dt in (F32, BF16)[:len(prev)]]
    specs = [src, src, yg_spec, gw_spec, _resident(w_gu.shape), _resident(w_d.shape), vec, vec]
    args = [x, xb, yg, gate_w, w_gu, w_d, g.reshape(1, d), b.reshape(1, d)]
    aliases = {}
    for o, buf in enumerate(prev):
        if buf is not None:
            specs.append(pl.BlockSpec(memory_space=pl.ANY))
            args.append(buf)
            aliases[len(args) - 1] = o
    return pl.pallas_call(
        functools.partial(_shared_ln_body, alpha=alpha, nout=len(prev)), grid=(n // tm,),
        in_specs=specs, out_specs=[dst] * len(out_shape), out_shape=out_shape, input_output_aliases=aliases,
        compiler_params=_params(("parallel",)), name="shared_expert_ln")(*args)


def _dispatch_plan(idx, n):
    a = n * TOP_K
    flat_e = idx.T.reshape(a)
    iota = jnp.arange(a, dtype=jnp.int32)
    _, order = lax.sort_key_val(flat_e, iota)
    _, rank = lax.sort_key_val(order, iota)
    experts = jnp.arange(N_EXPERTS, dtype=jnp.int32)
    counts = jnp.sum((flat_e[:, None] == experts[None, :]).astype(jnp.int32), axis=0)
    padded = (counts + MOE_ROWS - 1) // MOE_ROWS * MOE_ROWS
    pad_end = jnp.cumsum(padded)
    pad_start = pad_end - padded
    start = jnp.cumsum(counts) - counts
    n_blocks = a // MOE_ROWS + N_EXPERTS
    blk0 = jnp.arange(n_blocks, dtype=jnp.int32) * MOE_ROWS
    blk_e = jnp.minimum(jnp.sum((pad_end[None, :] <= blk0[:, None]).astype(jnp.int32), axis=1), N_EXPERTS - 1)
    src = (start - pad_start)[blk_e][:, None] + blk0[:, None] + jnp.arange(MOE_ROWS, dtype=jnp.int32)[None, :]
    slot_token = order[jnp.clip(src, 0, a - 1).reshape(-1)] // TOP_K
    nused = (pad_end[-1] // MOE_ROWS).astype(jnp.int32).reshape(1)
    slot_of = ((pad_start - start)[flat_e] + rank).reshape(n, TOP_K)
    return slot_token, blk_e.astype(jnp.int32), nused, slot_of


def _prep_layer(l, p):
    bf = lambda a: a.astype(BF16)
    w_in = p['w_in'][l]
    offs = np.cumsum((0, A_Q_HEADS * HEAD_DIM, A_KV_HEADS * HEAD_DIM, A_KV_HEADS * HEAD_DIM, B_Q_RANK, B_KV_RANK,
                      B_ROPE, C_Q_HEADS * HEAD_DIM, C_KV_HEADS * HEAD_DIM, C_KV_HEADS * HEAD_DIM, N_BRANCH * D_MODEL))
    piece = lambda i: w_in[:, offs[i]:offs[i + 1]]
    zeros = lambda c: jnp.zeros((D_MODEL, c), w_in.dtype)
    w_h = jnp.concatenate([piece(9), piece(0), piece(1), piece(2), piece(3), piece(4), piece(5), zeros(LANES - B_ROPE),
                           piece(6), piece(7), piece(8), zeros(H_COLS - H_USED)], axis=1)
    w_uq = p['w_uq'][l].reshape(B_Q_RANK, B_HEADS, B_NOPE + B_ROPE)
    w_uq_rope = jnp.pad(w_uq[:, :, B_NOPE:], ((0, 0), (0, 0), (0, LANES - B_ROPE)))
    w_q = jnp.concatenate([w_uq[:, :, :B_NOPE].reshape(B_Q_RANK, -1), w_uq_rope.reshape(B_Q_RANK, -1)], axis=1)
    w_ukv = p['w_ukv'][l].reshape(B_KV_RANK, B_HEADS, B_NOPE + B_V)
    w_k = w_ukv[:, :, :B_NOPE].reshape(B_KV_RANK, -1)
    w_v_t = w_ukv[:, :, B_NOPE:].reshape(B_KV_RANK, -1).T
    return dict(
        w_h=bf(w_h), w_q=bf(w_q), w_k=bf(w_k), w_v_t=bf(w_v_t), w_cv_t=bf(piece(8).T),
        w_o_a=bf(p['w_o_a'][l]), w_o_b=bf(p['w_o_b'][l]), w_o_c=bf(p['w_o_c'][l]), w_out=bf(p['w_out'][l]),
        w_s_gu=bf(jnp.concatenate([p['w_s_gate'][l], p['w_s_up'][l]], axis=-1)), w_s_d=bf(p['w_s_down'][l]),
        c_gains=jnp.concatenate([jnp.tile(p['c_q_norm'][l][None] * (HEAD_DIM ** -0.5 * LOG2E), (C_Q_HEADS, 1)),
                                 jnp.tile(p['c_k_norm'][l][None], (C_KV_HEADS, 1))], axis=0),
    )


def _token_mixer(x, xb, lw, l, p, tabs, segs, alpha):
    tab_a, tab_b, tab_c = tabs
    h = matmul(xb, lw['w_h'], name="in_proj")
    assert all((window // dilation) // 2 == A_RADIUS for window, dilation in A_PATTERNS)
    qk_a = rope(h, tab_a, name="rope_a", segs=segs, col=AQ_OFF, heads=A_Q_HEADS + A_KV_HEADS, shift=PARTIAL_ROT // 2)
    o_a = band_attention(qk_a, h, segs=segs)
    q_b = matmul(h, lw['w_q'], name="mla_q", x_col=BCQ_OFF, gain=p['mla_q_norm'][l],
                 out_scale=(B_NOPE + B_ROPE) ** -0.5 * LOG2E)
    k_b = matmul(h, lw['w_k'], name="mla_k", x_col=BCKV_OFF, gain=p['mla_kv_norm'][l])
    vt_b = matmul_t(h, lw['w_v_t'], name="mla_v_t", tm=FLASH_KEYS, x_col=BCKV_OFF, gain=p['mla_kv_norm'][l])
    q_b_rope = rope(q_b, tab_b, name="rope_bq", segs=segs, col=B_HEADS * LANES, heads=B_HEADS, shift=B_ROPE // 2)
    k_b_rope = rope(h, tab_b, name="rope_bk", segs=segs, col=BKR_OFF, heads=1, shift=B_ROPE // 2)
    o_b = flash_attention([(q_b, 0, LANES), (q_b_rope, 0, LANES)], [(k_b, 0, LANES), (k_b_rope, 0, 0)], vt_b,
                          segs=segs, kv_heads=B_HEADS, group=1, name="flash_b")
    qk_c = rope(h, tab_c, name="rope_c", segs=segs, col=CQ_OFF, heads=C_Q_HEADS + C_KV_HEADS, shift=HEAD_DIM // 4,
                gains=lw['c_gains'])
    vt_c = matmul_t(xb, lw['w_cv_t'], name="c_v_t", tm=FLASH_KEYS)
    group = C_Q_HEADS // C_KV_HEADS
    o_c = flash_attention([(qk_c, 0, group * LANES)], [(qk_c, C_Q_HEADS * LANES, LANES)], vt_c,
                          segs=segs, kv_heads=C_KV_HEADS, group=group, name="flash_c")
    merged = merge_mixers(o_a, o_b, o_c, h, lw['w_o_a'], lw['w_o_b'], lw['w_o_c'])
    return proj_residual_ln(merged, lw['w_out'], x, p['ln1_g'][l], p['ln1_b'][l], alpha=alpha)


MOE_CHUNK = 16384


def _moe(x, xb, lw, l, p, alpha, segs, final):
    n, d = x.shape
    idx, wts = router(x, p['w_router'][l], p['router_bias'][l])
    gate_w = wts.T
    if final:
        bufs = [(None,) for _ in segs]
    else:
        bufs = [(None, None)]
    for si, (seg0, batch, seq) in enumerate(segs):
        rows = batch * seq
        step = min(MOE_CHUNK, rows)
        assert rows % step == 0
        for r0 in range(seg0, seg0 + rows, step):
            slot_token, blk_e, nused, slot_of = _dispatch_plan(idx[:, r0:r0 + step], step)
            xs = xb.at[slot_token + r0].get(mode='promise_in_bounds')
            y = grouped_experts(xs, blk_e, nused, p['w_e_gate'], p['w_e_up'], p['w_e_down'], l)
            yg = y.at[slot_of.T].get(mode='promise_in_bounds')
            b = si if final else 0
            bufs[b] = tuple(shared_expert_ln(
                x, xb, yg, gate_w[r0:r0 + step], lw['w_s_gu'], lw['w_s_d'], p['ln2_g'][l], p['ln2_b'][l], alpha=alpha,
                row0=r0, out_row0=r0 - seg0 if final else r0, out_rows=rows if final else n, prev=bufs[b]))
    return tuple(b[0] for b in bufs) if final else bufs[0]


def _run_trunk(xs3, layers, p):
    d = xs3[0].shape[-1]
    segs, row0 = [], 0
    for x3 in xs3:
        segs.append((row0, x3.shape[0], x3.shape[1]))
        row0 += x3.shape[0] * x3.shape[1]
    segs = tuple(segs)
    alpha = (2 * len(layers)) ** 0.25
    tabs = _rope_tables(max(seq for _, _, seq in segs))
    x = jnp.concatenate([x3.reshape(-1, d) for x3 in xs3], axis=0)
    xb = x.astype(BF16)
    last = len(layers) - 1
    for l, lw in enumerate(layers):
        x, xb = _token_mixer(x, xb, lw, l, p, tabs, segs, alpha)
        if l < last:
            x, xb = _moe(x, xb, lw, l, p, alpha, segs, final=False)
    outs = _moe(x, xb, layers[last], last, p, alpha, segs, final=True)
    return tuple(o.reshape(b, s, d) for o, (_, b, s) in zip(outs, segs))


def kernel(x_prompt, x_sample, ln1_g, ln1_b, w_in, mla_q_norm, w_uq, mla_kv_norm, w_ukv, c_q_norm, c_k_norm, w_o_a, w_o_b, w_o_c, w_out, ln2_g, ln2_b, w_router, router_bias, w_e_gate, w_e_up, w_e_down, w_s_gate, w_s_up, w_s_down):
    p = dict(ln1_g=ln1_g, ln1_b=ln1_b, w_in=w_in, mla_q_norm=mla_q_norm, w_uq=w_uq, mla_kv_norm=mla_kv_norm,
             w_ukv=w_ukv, c_q_norm=c_q_norm, c_k_norm=c_k_norm, w_o_a=w_o_a, w_o_b=w_o_b, w_o_c=w_o_c, w_out=w_out,
             ln2_g=ln2_g, ln2_b=ln2_b, w_router=w_router, router_bias=router_bias, w_e_gate=w_e_gate,
             w_e_up=w_e_up, w_e_down=w_e_down, w_s_gate=w_s_gate, w_s_up=w_s_up, w_s_down=w_s_down)
    layers = [_prep_layer(l, p) for l in range(w_in.shape[0])]
    return _run_trunk((x_prompt, x_sample), layers, p)
```

```python
import functools

import numpy as np
import jax
import jax.numpy as jnp
from jax import lax
from jax.experimental import pallas as pl
from jax.experimental.pallas import tpu as pltpu

F32 = jnp.float32
BF16 = jnp.bfloat16

D_MODEL = 2048
HEAD_DIM = 128
A_PATTERNS = ((128, 1), (512, 4), (2048, 16))
A_KV_HEADS = 4
A_Q_HEADS = len(A_PATTERNS) * A_KV_HEADS
ROPE_THETA = 500000.0
PARTIAL_ROT = HEAD_DIM // 4
B_HEADS = 8
B_Q_RANK = 512
B_KV_RANK = 256
B_NOPE = 128
B_ROPE = 64
B_V = 128
C_Q_HEADS = 8
C_KV_HEADS = 2
AXIAL_THETA = 10000.0
GRID_W = 64
N_BRANCH = 3
N_EXPERTS = 64
TOP_K = 8
N_EXPERT_GROUPS = 8
TOPK_GROUPS = 4
EXPERT_FF = 512
SHARED_FF = 512
ROUTED_SCALE = 2.5
LN_EPS = 1e-5
RMS_EPS = 1e-6
MASK_VALUE = -1e30

LANES = 128
VMEM_LIMIT = 56 * 1024 * 1024
_NT = (((1,), (1,)), ((), ()))

GATE_OFF = 0
AQ_OFF = GATE_OFF + N_BRANCH * D_MODEL
AK_OFF = AQ_OFF + A_Q_HEADS * HEAD_DIM
AV_OFF = AK_OFF + A_KV_HEADS * HEAD_DIM
BCQ_OFF = AV_OFF + A_KV_HEADS * HEAD_DIM
BCKV_OFF = BCQ_OFF + B_Q_RANK
BKR_OFF = BCKV_OFF + B_KV_RANK
CQ_OFF = BKR_OFF + LANES
CK_OFF = CQ_OFF + C_Q_HEADS * HEAD_DIM
CV_OFF = CK_OFF + C_KV_HEADS * HEAD_DIM
H_USED = CV_OFF + C_KV_HEADS * HEAD_DIM
H_COLS = -(-H_USED // 1024) * 1024

MOE_ROWS = 256


def _tile(n, t):
    t = min(n, t)
    assert n % t == 0, (n, t)
    return t


def _params(sem):
    return pltpu.CompilerParams(dimension_semantics=sem, vmem_limit_bytes=VMEM_LIMIT)


def _resident(shape):
    nd = len(shape)
    return pl.BlockSpec(shape, lambda *_: (0,) * nd, pipeline_mode=pl.Buffered(1))


def _mm_body(x_ref, w_ref, o_ref):
    o_ref[...] = jnp.dot(x_ref[...], w_ref[...], preferred_element_type=F32).astype(o_ref.dtype)


def _mm_rms_body(x_ref, g_ref, w_ref, o_ref, *, out_scale):
    xf = x_ref[...].astype(F32)
    y = xf * lax.rsqrt(jnp.mean(xf * xf, -1, keepdims=True) + RMS_EPS) * g_ref[...]
    acc = jnp.dot(y.astype(BF16), w_ref[...], preferred_element_type=F32)
    o_ref[...] = (acc * out_scale).astype(o_ref.dtype)


def matmul(x, w, *, name, x_col=0, gain=None, out_scale=1.0, tm=1024, tn=1024, out_dtype=BF16):
    m = x.shape[0]
    k, n = w.shape
    tm, tn = _tile(m, tm), _tile(n, tn)
    assert x_col % k == 0
    x_spec = pl.BlockSpec((tm, k), lambda i, j: (i, x_col // k))
    w_spec = pl.BlockSpec((k, tn), lambda i, j: (0, j))
    o_spec = pl.BlockSpec((tm, tn), lambda i, j: (i, j))
    if gain is None:
        assert out_scale == 1.0
        body, specs, args = _mm_body, [x_spec, w_spec], (x, w)
    else:
        g_spec = pl.BlockSpec((1, k), lambda i, j: (0, 0))
        body = functools.partial(_mm_rms_body, out_scale=out_scale)
        specs, args = [x_spec, g_spec, w_spec], (x, gain.reshape(1, k).astype(F32), w)
    return pl.pallas_call(
        body, grid=(m // tm, n // tn), in_specs=specs, out_specs=o_spec,
        out_shape=jax.ShapeDtypeStruct((m, n), out_dtype),
        compiler_params=_params(("parallel", "arbitrary")), name=name)(*args)


def _mm_t_body(x_ref, w_ref, o_ref):
    o_ref[...] = lax.dot_general(w_ref[...], x_ref[...], _NT, preferred_element_type=F32).astype(o_ref.dtype)


def _mm_t_rms_body(x_ref, g_ref, w_ref, o_ref):
    xf = x_ref[...].astype(F32)
    y = xf * lax.rsqrt(jnp.mean(xf * xf, -1, keepdims=True) + RMS_EPS) * g_ref[...]
    o_ref[...] = lax.dot_general(w_ref[...], y.astype(BF16), _NT, preferred_element_type=F32).astype(o_ref.dtype)


def matmul_t(x, w_t, *, name, tm, x_col=0, gain=None):
    m = x.shape[0]
    n, k = w_t.shape
    assert m % tm == 0 and x_col % k == 0
    x_spec = pl.BlockSpec((tm, k), lambda i: (i, x_col // k))
    o_spec = pl.BlockSpec((None, n, tm), lambda i: (i, 0, 0))
    if gain is None:
        body, specs, args = _mm_t_body, [x_spec, _resident(w_t.shape)], (x, w_t)
    else:
        g_spec = pl.BlockSpec((1, k), lambda i: (0, 0))
        body, specs, args = _mm_t_rms_body, [x_spec, g_spec, _resident(w_t.shape)], (x, gain.reshape(1, k).astype(F32), w_t)
    return pl.pallas_call(
        body, grid=(m // tm,), in_specs=specs, out_specs=o_spec,
        out_shape=jax.ShapeDtypeStruct((m // tm, n, tm), BF16),
        compiler_params=_params(("parallel",)), name=name)(*args)


def _rope_body(*refs, shift, rms, heads):
    if rms:
        x_ref, c_ref, s1_ref, s2_ref, g_ref, o_ref = refs
    else:
        x_ref, c_ref, s1_ref, s2_ref, o_ref = refs
    c, s1, s2 = c_ref[...], s1_ref[...], s2_ref[...]
    for hd in range(heads):
        sl = slice(hd * LANES, (hd + 1) * LANES)
        x = x_ref[:, sl].astype(F32)
        if rms:
            x = x * lax.rsqrt(jnp.mean(x * x, -1, keepdims=True) + RMS_EPS) * g_ref[hd]
        out = x * c + pltpu.roll(x, LANES - shift, 1) * s1 + pltpu.roll(x, shift, 1) * s2
        o_ref[:, sl] = out.astype(o_ref.dtype)


def rope(x, tabs, *, name, segs, col, heads, shift, gains=None, ts=1024):
    n = x.shape[0]
    ts = min([ts] + [seq for _, _, seq in segs])
    assert n % ts == 0 and all(row0 % ts == 0 and seq % ts == 0 for row0, _, seq in segs)
    blk = col // LANES

    def tab_block(r):
        idx = r * 0
        for row0, _, seq in segs:
            idx = jnp.where(r >= row0 // ts, (r - row0 // ts) % (seq // ts), idx)
        return idx

    hp = max(c for c in range(1, 9) if heads % c == 0 and blk % c == 0)
    x_spec = pl.BlockSpec((ts, hp * LANES), lambda r, h: (r, blk // hp + h))
    t_spec = pl.BlockSpec((ts, LANES), lambda r, h: (tab_block(r), 0))
    o_spec = pl.BlockSpec((ts, hp * LANES), lambda r, h: (r, h))
    specs, args = [x_spec, t_spec, t_spec, t_spec], [x, *tabs]
    if gains is not None:
        specs.append(pl.BlockSpec((hp, 1, LANES), lambda r, h: (h, 0, 0)))
        args.append(gains.reshape(heads, 1, LANES).astype(F32))
    return pl.pallas_call(
        functools.partial(_rope_body, shift=shift, rms=gains is not None, heads=hp),
        grid=(n // ts, heads // hp), in_specs=specs, out_specs=o_spec,
        out_shape=jax.ShapeDtypeStruct((n, heads * LANES), BF16),
        compiler_params=_params(("parallel", "arbitrary")), name=name)(*args)


def _rope_tables(seq):
    def table(pos, dim, theta):
        inv = theta ** (-jnp.arange(0, dim, 2, dtype=F32) / dim)
        ang = pos.astype(F32)[:, None] * inv[None, :]
        return jnp.cos(ang), jnp.sin(ang)

    pos = jnp.arange(seq)
    row = pos // GRID_W
    colp = pos % GRID_W
    one = lambda n: jnp.ones((seq, n), F32)
    zero = lambda n: jnp.zeros((seq, n), F32)
    cat = lambda *xs: jnp.concatenate(xs, -1)
    ca, sa = table(pos, PARTIAL_ROT, ROPE_THETA)
    ra = LANES - PARTIAL_ROT
    tab_a = (cat(ca, ca, one(ra)), cat(-sa, zero(LANES - 16)), cat(zero(16), sa, zero(ra)))
    cb, sb = table(pos, B_ROPE, ROPE_THETA)
    tab_b = (cat(cb, cb, one(64)), cat(-sb, zero(96)), cat(zero(32), sb, zero(64)))
    cr, sr = table(row, HEAD_DIM // 2, AXIAL_THETA)
    cc, sc = table(colp, HEAD_DIM // 2, AXIAL_THETA)
    tab_c = (cat(cr, cr, cc, cc), cat(-sr, zero(32), -sc, zero(32)), cat(zero(32), sr, zero(32), sc))
    return tab_a, tab_b, tab_c


FLASH_KEYS = 512
LOG2E = 1.4426950408889634


FLASH_STRIP = 512
ONES_ROWS = 16


def _flash_body(*refs, nq, nk, group, tq, nchunks, aliased):
    q_refs, k_refs, vt_ref = refs[:nq], refs[nq:nq + nk], refs[nq + nk]
    o_ref, qt_scr, m_scr, mblk_scr, acc_scr, s_even, s_odd = refs[nq + nk + 1 + aliased:]
    if group > 1:
        for g in range(group):
            qt_scr[:, g * tq:(g + 1) * tq] = q_refs[0][:, g * LANES:(g + 1) * LANES].astype(F32).T.astype(BF16)
    else:
        for i, r in enumerate(q_refs):
            qt_scr[i * LANES:(i + 1) * LANES, :] = r[...].astype(F32).T.astype(BF16)
    nq_all = group * tq
    ones = jnp.ones((ONES_ROWS, FLASH_KEYS), BF16)

    def keys(j):
        off = pl.multiple_of(j * FLASH_KEYS, FLASH_KEYS)
        return jnp.concatenate([r[pl.ds(off, FLASH_KEYS), :] for r in k_refs], axis=1)

    s0 = jnp.dot(keys(0), qt_scr[...], preferred_element_type=F32)
    s_even[...] = s0
    mblk_scr[...] = jnp.max(s0, 0, keepdims=True)
    m_scr[...] = jnp.full(m_scr.shape, -jnp.inf, F32)
    acc_scr[...] = jnp.zeros(acc_scr.shape, F32)

    def block(j, s_cur, s_nxt):
        k_next = keys(jnp.minimum(j + 1, nchunks - 1))
        vt1 = jnp.concatenate([vt_ref[j], ones], axis=0)
        m_prev = m_scr[...]
        m_new = jnp.maximum(m_prev, mblk_scr[...])
        alpha = jnp.exp2(m_prev - m_new)
        m_scr[...] = m_new
        for st in range(nq_all // FLASH_STRIP):
            cs = slice(st * FLASH_STRIP, (st + 1) * FLASH_STRIP)
            s_n = jnp.dot(k_next, qt_scr[:, cs], preferred_element_type=F32)
            s_nxt[:, cs] = s_n
            mblk_scr[:, cs] = jnp.max(s_n, 0, keepdims=True)
            p = jnp.exp2(s_cur[:, cs] - m_new[:, cs]).astype(BF16)
            acc_scr[:, cs] = alpha[:, cs] * acc_scr[:, cs] + jnp.dot(vt1, p, preferred_element_type=F32)

    def pair(i, carry):
        block(2 * i, s_even, s_odd)
        block(2 * i + 1, s_odd, s_even)
        return carry

    lax.fori_loop(0, nchunks // 2, pair, 0, unroll=2 if nchunks % 4 == 0 else 1)
    acc = acc_scr[...]
    o_t = acc[:LANES] / acc[LANES:LANES + 1]
    for g in range(group):
        o_ref[:, g * LANES:(g + 1) * LANES] = o_t[:, g * tq:(g + 1) * tq].T.astype(o_ref.dtype)


def _alias_previous(specs, args, prev):
    if prev is None:
        return {}
    specs.append(pl.BlockSpec(memory_space=pl.ANY))
    args.append(prev)
    return {len(args) - 1: 0}


def flash_attention(q_srcs, k_srcs, v_t, *, segs, kv_heads, group, name, rows=1024):
    n = q_srcs[0][0].shape[0]
    nq, nk = len(q_srcs), len(k_srcs)
    qw = group * LANES
    dq = LANES if group > 1 else nq * LANES
    out = None
    for row0, batch, seq in segs:
        assert seq % (2 * FLASH_KEYS) == 0 and row0 % seq == 0
        tq = _tile(seq, rows // group)
        nqb, nchunks = seq // tq, seq // FLASH_KEYS
        r = group * tq
        assert r % FLASH_STRIP == 0
        qb0, sb0 = row0 // tq, row0 // seq
        specs, args = [], []
        for arr, col, stride in q_srcs:
            assert col % qw == 0 and stride % qw == 0
            specs.append(pl.BlockSpec((tq, qw), lambda b, h, i, c=col // qw, st=stride // qw, qb0=qb0, nqb=nqb:
                                      (qb0 + b * nqb + i, c + h * st)))
            args.append(arr)
        for arr, col, stride in k_srcs:
            specs.append(pl.BlockSpec((seq, LANES), lambda b, h, i, c=col // LANES, st=stride // LANES, sb0=sb0:
                                      (sb0 + b, c + h * st)))
            args.append(arr)
        specs.append(pl.BlockSpec((nchunks, LANES, FLASH_KEYS), lambda b, h, i, sb0=sb0: (sb0 + b, h, 0)))
        args.append(v_t)
        aliases = _alias_previous(specs, args, out)
        out = pl.pallas_call(
            functools.partial(_flash_body, nq=nq, nk=nk, group=group, tq=tq, nchunks=nchunks, aliased=out is not None),
            grid=(batch, kv_heads, nqb), in_specs=specs,
            out_specs=pl.BlockSpec((tq, qw), lambda b, h, i, qb0=qb0, nqb=nqb: (qb0 + b * nqb + i, h)),
            out_shape=jax.ShapeDtypeStruct((n, kv_heads * qw), BF16),
            scratch_shapes=[pltpu.VMEM((dq, r), BF16),
                            pltpu.VMEM((1, r), F32), pltpu.VMEM((1, r), F32),
                            pltpu.VMEM((LANES + ONES_ROWS, r), F32),
                            pltpu.VMEM((FLASH_KEYS, r), F32), pltpu.VMEM((FLASH_KEYS, r), F32)],
            input_output_aliases=aliases,
            compiler_params=_params(("parallel", "arbitrary", "arbitrary")), name=name)(*args)
    return out


A_QB = 256
A_RADIUS = 64


def _band_window(seq, dilation):
    return min(A_QB + 2 * A_RADIUS * dilation, seq)


OFF_PHASE = 1e9


def _band_body(*refs, seq, dilations, scale, aliased):
    ng = len(dilations)
    q_refs, k_ref, v_ref = refs[:ng], refs[ng], refs[ng + 1]
    rel_refs = refs[ng + 2:2 * ng + 2]
    o_ref = refs[2 * ng + 2 + aliased]
    t0 = pl.program_id(2) * A_QB
    outs, lses = [], []
    for g, d in enumerate(dilations):
        w = _band_window(seq, d)
        start = pl.multiple_of(jnp.clip(t0 - A_RADIUS * d, 0, seq - w), A_RADIUS)
        kw = k_ref[pl.ds(start, w), :]
        vw = v_ref[pl.ds(start, w), :]
        s = lax.dot_general(q_refs[g][...], kw, _NT, preferred_element_type=F32) * scale
        rel = rel_refs[g][...] + (start - t0).astype(F32)
        s = jnp.where(jnp.abs(rel) <= float(A_RADIUS * d), s, MASK_VALUE)
        m = jnp.max(s, -1, keepdims=True)
        p = jnp.exp(s - m)
        den = jnp.sum(p, -1, keepdims=True)
        outs.append(jnp.dot((p / den).astype(BF16), vw, preferred_element_type=F32))
        lses.append(m + jnp.log(den))
    mx = functools.reduce(jnp.maximum, lses)
    es = [jnp.exp(l - mx) for l in lses]
    tot = functools.reduce(lambda a, b: a + b, es)
    o_ref[...] = functools.reduce(lambda a, b: a + b, [(e / tot) * o for e, o in zip(es, outs)]).astype(o_ref.dtype)


def band_attention(qk, h, *, segs):
    n = qk.shape[0]
    dilations = tuple(d for _, d in A_PATTERNS)
    out = None
    for row0, batch, seq in segs:
        assert seq % A_QB == 0 and row0 % seq == 0
        nqb = seq // A_QB
        qb0, sb0 = row0 // A_QB, row0 // seq
        specs = [pl.BlockSpec((A_QB, LANES),
                              lambda b, hh, i, g=g, qb0=qb0, nqb=nqb: (qb0 + b * nqb + i, g * A_KV_HEADS + hh))
                 for g in range(len(dilations))]
        specs.append(pl.BlockSpec((seq, LANES), lambda b, hh, i, sb0=sb0: (sb0 + b, A_Q_HEADS + hh)))
        specs.append(pl.BlockSpec((seq, LANES), lambda b, hh, i, sb0=sb0: (sb0 + b, AV_OFF // LANES + hh)))
        args = [qk] * len(dilations) + [qk, h]
        for d in dilations:
            ji = np.arange(_band_window(seq, d))[None, :] - np.arange(A_QB)[:, None]
            rel = jnp.asarray(np.where(ji % d == 0, ji, OFF_PHASE), F32)
            specs.append(_resident(rel.shape))
            args.append(rel)
        aliases = _alias_previous(specs, args, out)
        out = pl.pallas_call(
            functools.partial(_band_body, seq=seq, dilations=dilations, scale=HEAD_DIM ** -0.5,
                              aliased=out is not None),
            grid=(batch, A_KV_HEADS, nqb), in_specs=specs,
            out_specs=pl.BlockSpec((A_QB, LANES), lambda b, hh, i, qb0=qb0, nqb=nqb: (qb0 + b * nqb + i, hh)),
            out_shape=jax.ShapeDtypeStruct((n, A_KV_HEADS * LANES), BF16),
            input_output_aliases=aliases,
            compiler_params=_params(("parallel", "arbitrary", "arbitrary")), name="band_attention")(*args)
    return out


def _sigmoid(x):
    return 1.0 / (1.0 + jnp.exp(-x))


def _layer_norm(y, g, b):
    mu = jnp.mean(y, -1, keepdims=True)
    yc = y - mu
    var = jnp.mean(yc * yc, -1, keepdims=True)
    return yc * lax.rsqrt(var + LN_EPS) * g + b


def _merge_body(oa_ref, ob_ref, oc_ref, g0_ref, g1_ref, g2_ref, wa_ref, wb_ref, wc_ref, o_ref):
    ya = jnp.dot(oa_ref[...], wa_ref[...], preferred_element_type=F32)
    yb = jnp.dot(ob_ref[...], wb_ref[...], preferred_element_type=F32)
    yc = jnp.dot(oc_ref[...], wc_ref[...], preferred_element_type=F32)
    merged = (_sigmoid(g0_ref[...].astype(F32)) * ya + _sigmoid(g1_ref[...].astype(F32)) * yb
              + _sigmoid(g2_ref[...].astype(F32)) * yc)
    o_ref[...] = merged.astype(o_ref.dtype)


def merge_mixers(oa, ob, oc, h, w_o_a, w_o_b, w_o_c, *, tm=256):
    n = ob.shape[0]
    tm = _tile(n, tm)
    row = lambda w: pl.BlockSpec((tm, w), lambda i: (i, 0))
    gate = lambda j: pl.BlockSpec((tm, D_MODEL), lambda i: (i, GATE_OFF // D_MODEL + j))
    specs = ([row(oa.shape[1]), row(ob.shape[1]), row(oc.shape[1]), gate(0), gate(1), gate(2)]
             + [_resident(w_o_a.shape), _resident(w_o_b.shape), _resident(w_o_c.shape)])
    return pl.pallas_call(
        _merge_body, grid=(n // tm,), in_specs=specs, out_specs=row(D_MODEL),
        out_shape=jax.ShapeDtypeStruct((n, D_MODEL), BF16),
        compiler_params=_params(("parallel",)), name="merge_mixers")(oa, ob, oc, h, h, h, w_o_a, w_o_b, w_o_c)


def _proj_ln_body(a_ref, w_ref, x_ref, g_ref, b_ref, o_ref, ob_ref, *, alpha):
    y = alpha * x_ref[...] + jnp.dot(a_ref[...], w_ref[...], preferred_element_type=F32)
    out = _layer_norm(y, g_ref[...], b_ref[...])
    o_ref[...] = out
    ob_ref[...] = out.astype(BF16)


def proj_residual_ln(a, w, x, g, b, *, alpha, tm=256):
    n, d = x.shape
    tm = _tile(n, tm)
    row = lambda wd: pl.BlockSpec((tm, wd), lambda i: (i, 0))
    vec = pl.BlockSpec((1, d), lambda i: (0, 0))
    return pl.pallas_call(
        functools.partial(_proj_ln_body, alpha=alpha), grid=(n // tm,),
        in_specs=[row(a.shape[1]), _resident(w.shape), row(d), vec, vec],
        out_specs=[row(d), row(d)],
        out_shape=[jax.ShapeDtypeStruct((n, d), F32), jax.ShapeDtypeStruct((n, d), BF16)],
        compiler_params=_params(("parallel",)), name="proj_residual_ln")(a, w, x, g.reshape(1, d), b.reshape(1, d))


def _split_bf16(x):
    hi = x.astype(BF16)
    lo = (x - hi.astype(F32)).astype(BF16)
    return hi, lo


def _router_body(x_ref, wh_ref, wl_ref, bias_ref, idx_ref, wt_ref):
    xh, xl = _split_bf16(x_ref[...])
    nt = (((1,), (1,)), ((), ()))
    wh, wl = wh_ref[...], wl_ref[...]
    logits = (lax.dot_general(wh, xh, nt, preferred_element_type=F32)
              + (lax.dot_general(wh, xl, nt, preferred_element_type=F32)
                 + lax.dot_general(wl, xh, nt, preferred_element_type=F32)))
    scores = _sigmoid(logits)
    biased = scores + bias_ref[...]
    tn = scores.shape[1]
    gsz = N_EXPERTS // N_EXPERT_GROUPS
    neg = -jnp.inf
    sub = lax.broadcasted_iota(jnp.int32, (gsz, tn), 0).astype(F32)
    gs = []
    for g in range(N_EXPERT_GROUPS):
        blk = biased[g * gsz:(g + 1) * gsz]
        m1 = jnp.max(blk, 0, keepdims=True)
        first = jnp.min(jnp.where(blk == m1, sub, float(gsz)), 0, keepdims=True)
        m2 = jnp.max(jnp.where(sub == first, neg, blk), 0, keepdims=True)
        gs.append(m1 + m2)
    gs = jnp.concatenate(gs, axis=0)
    gi = lax.broadcasted_iota(jnp.int32, (N_EXPERT_GROUPS, tn), 0).astype(F32)
    gsel = jnp.zeros((N_EXPERT_GROUPS, tn), F32)
    for _ in range(TOPK_GROUPS):
        gm = jnp.max(gs, 0, keepdims=True)
        pick = gi == jnp.min(jnp.where(gs == gm, gi, float(N_EXPERT_GROUPS)), 0, keepdims=True)
        gsel = jnp.where(pick, 1.0, gsel)
        gs = jnp.where(pick, neg, gs)
    masked = jnp.concatenate(
        [jnp.where(gsel[g:g + 1] > 0.0, biased[g * gsz:(g + 1) * gsz], neg) for g in range(N_EXPERT_GROUPS)], axis=0)
    ei = lax.broadcasted_iota(jnp.int32, (N_EXPERTS, tn), 0).astype(F32)
    idxs, wts = [], []
    for _ in range(TOP_K):
        mx = jnp.max(masked, 0, keepdims=True)
        ix = jnp.min(jnp.where(masked == mx, ei, float(N_EXPERTS)), 0, keepdims=True)
        pick = ei == ix
        idxs.append(ix)
        wts.append(jnp.sum(jnp.where(pick, scores, 0.0), 0, keepdims=True))
        masked = jnp.where(pick, neg, masked)
    w = jnp.concatenate(wts, axis=0)
    idx_ref[...] = jnp.concatenate(idxs, axis=0).astype(jnp.int32)
    wt_ref[...] = w / jnp.sum(w, 0, keepdims=True) * ROUTED_SCALE


def router(x, w_router, router_bias, *, tn=512):
    n, d = x.shape
    tn = _tile(n, tn)
    wt = w_router.T.astype(F32)
    wh = wt.astype(BF16)
    wl = (wt - wh.astype(F32)).astype(BF16)
    full = lambda r, c: pl.BlockSpec((r, c), lambda i: (0, 0))
    out = pl.BlockSpec((TOP_K, tn), lambda i: (0, i))
    return pl.pallas_call(
        _router_body, grid=(n // tn,),
        in_specs=[pl.BlockSpec((tn, d), lambda i: (i, 0)), full(N_EXPERTS, d), full(N_EXPERTS, d), full(N_EXPERTS, 1)],
        out_specs=[out, out],
        out_shape=[jax.ShapeDtypeStruct((TOP_K, n), jnp.int32), jax.ShapeDtypeStruct((TOP_K, n), F32)],
        compiler_params=_params(("parallel",)), name="router")(x, wh, wl, router_bias.reshape(N_EXPERTS, 1).astype(F32))


def _silu(x):
    return x * _sigmoid(x)


def _experts_body(blk_e_ref, nused_ref, x_ref, wg_ref, wu_ref, wd_ref, o_ref, wgu_scr, wd_scr):
    i = pl.program_id(0)
    used = i < nused_ref[0]
    new_expert = blk_e_ref[i] != blk_e_ref[jnp.maximum(i - 1, 0)]

    @pl.when(used & ((i == 0) | new_expert))
    def _():
        wgu_scr[:, :EXPERT_FF] = wg_ref[...].astype(BF16)
        wgu_scr[:, EXPERT_FF:] = wu_ref[...].astype(BF16)
        wd_scr[...] = wd_ref[...].astype(BF16)

    @pl.when(used)
    def _():
        hgu = jnp.dot(x_ref[...], wgu_scr[...], preferred_element_type=F32)
        a = _silu(hgu[:, :EXPERT_FF]) * hgu[:, EXPERT_FF:]
        o_ref[...] = jnp.dot(a.astype(BF16), wd_scr[...], preferred_element_type=F32).astype(o_ref.dtype)

    @pl.when(jnp.logical_not(used))
    def _():
        o_ref[...] = jnp.zeros(o_ref.shape, o_ref.dtype)


def grouped_experts(xs, blk_e, nused, w_gate, w_up, w_down, layer):
    p, d = xs.shape
    nb = p // MOE_ROWS
    grid_spec = pltpu.PrefetchScalarGridSpec(
        num_scalar_prefetch=2, grid=(nb,),
        in_specs=[pl.BlockSpec((MOE_ROWS, d), lambda i, e, u: (i, 0)),
                  pl.BlockSpec((None, None, d, EXPERT_FF), lambda i, e, u: (layer, e[i], 0, 0)),
                  pl.BlockSpec((None, None, d, EXPERT_FF), lambda i, e, u: (layer, e[i], 0, 0)),
                  pl.BlockSpec((None, None, EXPERT_FF, d), lambda i, e, u: (layer, e[i], 0, 0))],
        out_specs=pl.BlockSpec((MOE_ROWS, d), lambda i, e, u: (i, 0)),
        scratch_shapes=[pltpu.VMEM((d, 2 * EXPERT_FF), BF16), pltpu.VMEM((EXPERT_FF, d), BF16)])
    return pl.pallas_call(
        _experts_body, grid_spec=grid_spec, out_shape=jax.ShapeDtypeStruct((p, d), BF16),
        compiler_params=_params(("arbitrary",)), name="grouped_experts")(blk_e, nused, xs, w_gate, w_up, w_down)


def _shared_ln_body(*refs, alpha, nout):
    x_ref, xb_ref, yg_ref, gw_ref, wgu_ref, wd_ref, g_ref, b_ref = refs[:8]
    outs = refs[len(refs) - nout:]
    o_ref, ob_ref = outs[0], outs[1:]
    hgu = jnp.dot(xb_ref[...], wgu_ref[...], preferred_element_type=F32)
    a = _silu(hgu[:, :SHARED_FF]) * hgu[:, SHARED_FF:]
    y = jnp.dot(a.astype(BF16), wd_ref[...], preferred_element_type=F32)
    gw = gw_ref[...]
    for k in range(TOP_K):
        y = y + yg_ref[k].astype(F32) * gw[:, k:k + 1]
    out = _layer_norm(alpha * x_ref[...] + y, g_ref[...], b_ref[...])
    o_ref[...] = out
    for ref in ob_ref:
        ref[...] = out.astype(BF16)


def shared_expert_ln(x, xb, yg, gate_w, w_gu, w_d, g, b, *, alpha, row0, out_row0, out_rows, prev, tm=256):
    d = x.shape[1]
    n = yg.shape[1]
    tm = _tile(n, tm)
    assert row0 % tm == 0 and out_row0 % tm == 0
    src = pl.BlockSpec((tm, d), lambda i: (row0 // tm + i, 0))
    yg_spec = pl.BlockSpec((TOP_K, tm, d), lambda i: (0, i, 0))
    gw_spec = pl.BlockSpec((tm, TOP_K), lambda i: (i, 0))
    dst = pl.BlockSpec((tm, d), lambda i: (out_row0 // tm + i, 0))
    vec = pl.BlockSpec((1, d), lambda i: (0, 0))
    out_shape = [jax.ShapeDtypeStruct((out_rows, d), dt) for dt in (F32, BF16)[:len(prev)]]
    specs = [src, src, yg_spec, gw_spec, _resident(w_gu.shape), _resident(w_d.shape), vec, vec]
    args = [x, xb, yg, gate_w, w_gu, w_d, g.reshape(1, d), b.reshape(1, d)]
    aliases = {}
    for o, buf in enumerate(prev):
        if buf is not None:
            specs.append(pl.BlockSpec(memory_space=pl.ANY))
            args.append(buf)
            aliases[len(args) - 1] = o
    return pl.pallas_call(
        functools.partial(_shared_ln_body, alpha=alpha, nout=len(prev)), grid=(n // tm,),
        in_specs=specs, out_specs=[dst] * len(out_shape), out_shape=out_shape, input_output_aliases=aliases,
        compiler_params=_params(("parallel",)), name="shared_expert_ln")(*args)


def _dispatch_plan(idx, n):
    a = n * TOP_K
    flat_e = idx.T.reshape(a)
    iota = jnp.arange(a, dtype=jnp.int32)
    _, order = lax.sort_key_val(flat_e, iota)
    _, rank = lax.sort_key_val(order, iota)
    experts = jnp.arange(N_EXPERTS, dtype=jnp.int32)
    counts = jnp.sum((flat_e[:, None] == experts[None, :]).astype(jnp.int32), axis=0)
    padded = (counts + MOE_ROWS - 1) // MOE_ROWS * MOE_ROWS
    pad_end = jnp.cumsum(padded)
    pad_start = pad_end - padded
    start = jnp.cumsum(counts) - counts
    n_blocks = a // MOE_ROWS + N_EXPERTS
    blk0 = jnp.arange(n_blocks, dtype=jnp.int32) * MOE_ROWS
    blk_e = jnp.minimum(jnp.sum((pad_end[None, :] <= blk0[:, None]).astype(jnp.int32), axis=1), N_EXPERTS - 1)
    src = (start - pad_start)[blk_e][:, None] + blk0[:, None] + jnp.arange(MOE_ROWS, dtype=jnp.int32)[None, :]
    slot_token = order[jnp.clip(src, 0, a - 1).reshape(-1)] // TOP_K
    nused = (pad_end[-1] // MOE_ROWS).astype(jnp.int32).reshape(1)
    slot_of = ((pad_start - start)[flat_e] + rank).reshape(n, TOP_K)
    return slot_token, blk_e.astype(jnp.int32), nused, slot_of


def _prep_layer(l, p):
    bf = lambda a: a.astype(BF16)
    w_in = p['w_in'][l]
    offs = np.cumsum((0, A_Q_HEADS * HEAD_DIM, A_KV_HEADS * HEAD_DIM, A_KV_HEADS * HEAD_DIM, B_Q_RANK, B_KV_RANK,
                      B_ROPE, C_Q_HEADS * HEAD_DIM, C_KV_HEADS * HEAD_DIM, C_KV_HEADS * HEAD_DIM, N_BRANCH * D_MODEL))
    piece = lambda i: w_in[:, offs[i]:offs[i + 1]]
    zeros = lambda c: jnp.zeros((D_MODEL, c), w_in.dtype)
    w_h = jnp.concatenate([piece(9), piece(0), piece(1), piece(2), piece(3), piece(4), piece(5), zeros(LANES - B_ROPE),
                           piece(6), piece(7), piece(8), zeros(H_COLS - H_USED)], axis=1)
    w_uq = p['w_uq'][l].reshape(B_Q_RANK, B_HEADS, B_NOPE + B_ROPE)
    w_uq_rope = jnp.pad(w_uq[:, :, B_NOPE:], ((0, 0), (0, 0), (0, LANES - B_ROPE)))
    w_q = jnp.concatenate([w_uq[:, :, :B_NOPE].reshape(B_Q_RANK, -1), w_uq_rope.reshape(B_Q_RANK, -1)], axis=1)
    w_ukv = p['w_ukv'][l].reshape(B_KV_RANK, B_HEADS, B_NOPE + B_V)
    w_k = w_ukv[:, :, :B_NOPE].reshape(B_KV_RANK, -1)
    w_v_t = w_ukv[:, :, B_NOPE:].reshape(B_KV_RANK, -1).T
    return dict(
        w_h=bf(w_h), w_q=bf(w_q), w_k=bf(w_k), w_v_t=bf(w_v_t), w_cv_t=bf(piece(8).T),
        w_o_a=bf(p['w_o_a'][l]), w_o_b=bf(p['w_o_b'][l]), w_o_c=bf(p['w_o_c'][l]), w_out=bf(p['w_out'][l]),
        w_s_gu=bf(jnp.concatenate([p['w_s_gate'][l], p['w_s_up'][l]], axis=-1)), w_s_d=bf(p['w_s_down'][l]),
        c_gains=jnp.concatenate([jnp.tile(p['c_q_norm'][l][None] * (HEAD_DIM ** -0.5 * LOG2E), (C_Q_HEADS, 1)),
                                 jnp.tile(p['c_k_norm'][l][None], (C_KV_HEADS, 1))], axis=0),
    )


def _token_mixer(x, xb, lw, l, p, tabs, segs, alpha):
    tab_a, tab_b, tab_c = tabs
    h = matmul(xb, lw['w_h'], name="in_proj")
    assert all((window // dilation) // 2 == A_RADIUS for window, dilation in A_PATTERNS)
    qk_a = rope(h, tab_a, name="rope_a", segs=segs, col=AQ_OFF, heads=A_Q_HEADS + A_KV_HEADS, shift=PARTIAL_ROT // 2)
    o_a = band_attention(qk_a, h, segs=segs)
    q_b = matmul(h, lw['w_q'], name="mla_q", x_col=BCQ_OFF, gain=p['mla_q_norm'][l],
                 out_scale=(B_NOPE + B_ROPE) ** -0.5 * LOG2E)
    k_b = matmul(h, lw['w_k'], name="mla_k", x_col=BCKV_OFF, gain=p['mla_kv_norm'][l])
    vt_b = matmul_t(h, lw['w_v_t'], name="mla_v_t", tm=FLASH_KEYS, x_col=BCKV_OFF, gain=p['mla_kv_norm'][l])
    q_b_rope = rope(q_b, tab_b, name="rope_bq", segs=segs, col=B_HEADS * LANES, heads=B_HEADS, shift=B_ROPE // 2)
    k_b_rope = rope(h, tab_b, name="rope_bk", segs=segs, col=BKR_OFF, heads=1, shift=B_ROPE // 2)
    o_b = flash_attention([(q_b, 0, LANES), (q_b_rope, 0, LANES)], [(k_b, 0, LANES), (k_b_rope, 0, 0)], vt_b,
                          segs=segs, kv_heads=B_HEADS, group=1, name="flash_b")
    qk_c = rope(h, tab_c, name="rope_c", segs=segs, col=CQ_OFF, heads=C_Q_HEADS + C_KV_HEADS, shift=HEAD_DIM // 4,
                gains=lw['c_gains'])
    vt_c = matmul_t(xb, lw['w_cv_t'], name="c_v_t", tm=FLASH_KEYS)
    group = C_Q_HEADS // C_KV_HEADS
    o_c = flash_attention([(qk_c, 0, group * LANES)], [(qk_c, C_Q_HEADS * LANES, LANES)], vt_c,
                          segs=segs, kv_heads=C_KV_HEADS, group=group, name="flash_c")
    merged = merge_mixers(o_a, o_b, o_c, h, lw['w_o_a'], lw['w_o_b'], lw['w_o_c'])
    return proj_residual_ln(merged, lw['w_out'], x, p['ln1_g'][l], p['ln1_b'][l], alpha=alpha)


MOE_CHUNK = 16384


def _moe(x, xb, lw, l, p, alpha, segs, final):
    n, d = x.shape
    idx, wts = router(x, p['w_router'][l], p['router_bias'][l])
    gate_w = wts.T
    if final:
        bufs = [(None,) for _ in segs]
    else:
        bufs = [(None, None)]
    for si, (seg0, batch, seq) in enumerate(segs):
        rows = batch * seq
        step = min(MOE_CHUNK, rows)
        assert rows % step == 0
        for r0 in range(seg0, seg0 + rows, step):
            slot_token, blk_e, nused, slot_of = _dispatch_plan(idx[:, r0:r0 + step], step)
            xs = xb.at[slot_token + r0].get(mode='promise_in_bounds')
            y = grouped_experts(xs, blk_e, nused, p['w_e_gate'], p['w_e_up'], p['w_e_down'], l)
            yg = y.at[slot_of.T].get(mode='promise_in_bounds')
            b = si if final else 0
            bufs[b] = tuple(shared_expert_ln(
                x, xb, yg, gate_w[r0:r0 + step], lw['w_s_gu'], lw['w_s_d'], p['ln2_g'][l], p['ln2_b'][l], alpha=alpha,
                row0=r0, out_row0=r0 - seg0 if final else r0, out_rows=rows if final else n, prev=bufs[b]))
    return tuple(b[0] for b in bufs) if final else bufs[0]


def _run_trunk(xs3, layers, p):
    d = xs3[0].shape[-1]
    segs, row0 = [], 0
    for x3 in xs3:
        segs.append((row0, x3.shape[0], x3.shape[1]))
        row0 += x3.shape[0] * x3.shape[1]
    segs = tuple(segs)
    alpha = (2 * len(layers)) ** 0.25
    tabs = _rope_tables(max(seq for _, _, seq in segs))
    x = jnp.concatenate([x3.reshape(-1, d) for x3 in xs3], axis=0)
    xb = x.astype(BF16)
    last = len(layers) - 1
    for l, lw in enumerate(layers):
        x, xb = _token_mixer(x, xb, lw, l, p, tabs, segs, alpha)
        if l < last:
            x, xb = _moe(x, xb, lw, l, p, alpha, segs, final=False)
    outs = _moe(x, xb, layers[last], last, p, alpha, segs, final=True)
    return tuple(o.reshape(b, s, d) for o, (_, b, s) in zip(outs, segs))


def kernel(x_prompt, x_sample, ln1_g, ln1_b, w_in, mla_q_norm, w_uq, mla_kv_norm, w_ukv, c_q_norm, c_k_norm, w_o_a, w_o_b, w_o_c, w_out, ln2_g, ln2_b, w_router, router_bias, w_e_gate, w_e_up, w_e_down, w_s_gate, w_s_up, w_s_down):
    p = dict(ln1_g=ln1_g, ln1_b=ln1_b, w_in=w_in, mla_q_norm=mla_q_norm, w_uq=w_uq, mla_kv_norm=mla_kv_norm,
             w_ukv=w_ukv, c_q_norm=c_q_norm, c_k_norm=c_k_norm, w_o_a=w_o_a, w_o_b=w_o_b, w_o_c=w_o_c, w_out=w_out,
             ln2_g=ln2_g, ln2_b=ln2_b, w_router=w_router, router_bias=router_bias, w_e_gate=w_e_gate,
             w_e_up=w_e_up, w_e_down=w_e_down, w_s_gate=w_s_gate, w_s_up=w_s_up, w_s_down=w_s_down)
    layers = [_prep_layer(l, p) for l in range(w_in.shape[0])]
    return _run_trunk((x_prompt, x_sample), layers, p)
```

```python
import functools

import numpy as np
import jax
import jax.numpy as jnp
from jax import lax
from jax.experimental import pallas as pl
from jax.experimental.pallas import tpu as pltpu

F32 = jnp.float32
BF16 = jnp.bfloat16

D_MODEL = 2048
HEAD_DIM = 128
A_PATTERNS = ((128, 1), (512, 4), (2048, 16))
A_KV_HEADS = 4
A_Q_HEADS = len(A_PATTERNS) * A_KV_HEADS
ROPE_THETA = 500000.0
PARTIAL_ROT = HEAD_DIM // 4
B_HEADS = 8
B_Q_RANK = 512
B_KV_RANK = 256
B_NOPE = 128
B_ROPE = 64
B_V = 128
C_Q_HEADS = 8
C_KV_HEADS = 2
AXIAL_THETA = 10000.0
GRID_W = 64
N_BRANCH = 3
N_EXPERTS = 64
TOP_K = 8
N_EXPERT_GROUPS = 8
TOPK_GROUPS = 4
EXPERT_FF = 512
SHARED_FF = 512
ROUTED_SCALE = 2.5
LN_EPS = 1e-5
RMS_EPS = 1e-6
MASK_VALUE = -1e30

LANES = 128
VMEM_LIMIT = 56 * 1024 * 1024
_NT = (((1,), (1,)), ((), ()))

GATE_OFF = 0
AQ_OFF = GATE_OFF + N_BRANCH * D_MODEL
AK_OFF = AQ_OFF + A_Q_HEADS * HEAD_DIM
AV_OFF = AK_OFF + A_KV_HEADS * HEAD_DIM
BCQ_OFF = AV_OFF + A_KV_HEADS * HEAD_DIM
BCKV_OFF = BCQ_OFF + B_Q_RANK
BKR_OFF = BCKV_OFF + B_KV_RANK
CQ_OFF = BKR_OFF + LANES
CK_OFF = CQ_OFF + C_Q_HEADS * HEAD_DIM
CV_OFF = CK_OFF + C_KV_HEADS * HEAD_DIM
H_USED = CV_OFF + C_KV_HEADS * HEAD_DIM
H_COLS = -(-H_USED // 1024) * 1024

MOE_ROWS = 256


def _tile(n, t):
    t = min(n, t)
    assert n % t == 0, (n, t)
    return t


def _params(sem):
    return pltpu.CompilerParams(dimension_semantics=sem, vmem_limit_bytes=VMEM_LIMIT)


def _resident(shape):
    nd = len(shape)
    return pl.BlockSpec(shape, lambda *_: (0,) * nd, pipeline_mode=pl.Buffered(1))


def _mm_body(x_ref, w_ref, o_ref):
    o_ref[...] = jnp.dot(x_ref[...], w_ref[...], preferred_element_type=F32).astype(o_ref.dtype)


def _mm_rms_body(x_ref, g_ref, w_ref, o_ref, *, out_scale):
    xf = x_ref[...].astype(F32)
    y = xf * lax.rsqrt(jnp.mean(xf * xf, -1, keepdims=True) + RMS_EPS) * g_ref[...]
    acc = jnp.dot(y.astype(BF16), w_ref[...], preferred_element_type=F32)
    o_ref[...] = (acc * out_scale).astype(o_ref.dtype)


def matmul(x, w, *, name, x_col=0, gain=None, out_scale=1.0, tm=1024, tn=1024, out_dtype=BF16):
    m = x.shape[0]
    k, n = w.shape
    tm, tn = _tile(m, tm), _tile(n, tn)
    assert x_col % k == 0
    x_spec = pl.BlockSpec((tm, k), lambda i, j: (i, x_col // k))
    w_spec = pl.BlockSpec((k, tn), lambda i, j: (0, j))
    o_spec = pl.BlockSpec((tm, tn), lambda i, j: (i, j))
    if gain is None:
        assert out_scale == 1.0
        body, specs, args = _mm_body, [x_spec, w_spec], (x, w)
    else:
        g_spec = pl.BlockSpec((1, k), lambda i, j: (0, 0))
        body = functools.partial(_mm_rms_body, out_scale=out_scale)
        specs, args = [x_spec, g_spec, w_spec], (x, gain.reshape(1, k).astype(F32), w)
    return pl.pallas_call(
        body, grid=(m // tm, n // tn), in_specs=specs, out_specs=o_spec,
        out_shape=jax.ShapeDtypeStruct((m, n), out_dtype),
        compiler_params=_params(("parallel", "arbitrary")), name=name)(*args)


def _mm_t_body(x_ref, w_ref, o_ref):
    o_ref[...] = lax.dot_general(w_ref[...], x_ref[...], _NT, preferred_element_type=F32).astype(o_ref.dtype)


def _mm_t_rms_body(x_ref, g_ref, w_ref, o_ref):
    xf = x_ref[...].astype(F32)
    y = xf * lax.rsqrt(jnp.mean(xf * xf, -1, keepdims=True) + RMS_EPS) * g_ref[...]
    o_ref[...] = lax.dot_general(w_ref[...], y.astype(BF16), _NT, preferred_element_type=F32).astype(o_ref.dtype)


def matmul_t(x, w_t, *, name, tm, x_col=0, gain=None):
    m = x.shape[0]
    n, k = w_t.shape
    assert m % tm == 0 and x_col % k == 0
    x_spec = pl.BlockSpec((tm, k), lambda i: (i, x_col // k))
    o_spec = pl.BlockSpec((None, n, tm), lambda i: (i, 0, 0))
    if gain is None:
        body, specs, args = _mm_t_body, [x_spec, _resident(w_t.shape)], (x, w_t)
    else:
        g_spec = pl.BlockSpec((1, k), lambda i: (0, 0))
        body, specs, args = _mm_t_rms_body, [x_spec, g_spec, _resident(w_t.shape)], (x, gain.reshape(1, k).astype(F32), w_t)
    return pl.pallas_call(
        body, grid=(m // tm,), in_specs=specs, out_specs=o_spec,
        out_shape=jax.ShapeDtypeStruct((m // tm, n, tm), BF16),
        compiler_params=_params(("parallel",)), name=name)(*args)


def _rope_body(*refs, shift, rms, heads):
    if rms:
        x_ref, c_ref, s1_ref, s2_ref, g_ref, o_ref = refs
    else:
        x_ref, c_ref, s1_ref, s2_ref, o_ref = refs
    c, s1, s2 = c_ref[...], s1_ref[...], s2_ref[...]
    for hd in range(heads):
        sl = slice(hd * LANES, (hd + 1) * LANES)
        x = x_ref[:, sl].astype(F32)
        if rms:
            x = x * lax.rsqrt(jnp.mean(x * x, -1, keepdims=True) + RMS_EPS) * g_ref[hd]
        out = x * c + pltpu.roll(x, LANES - shift, 1) * s1 + pltpu.roll(x, shift, 1) * s2
        o_ref[:, sl] = out.astype(o_ref.dtype)


def rope(x, tabs, *, name, segs, col, heads, shift, gains=None, ts=1024):
    n = x.shape[0]
    ts = min([ts] + [seq for _, _, seq in segs])
    assert n % ts == 0 and all(row0 % ts == 0 and seq % ts == 0 for row0, _, seq in segs)
    blk = col // LANES

    def tab_block(r):
        idx = r * 0
        for row0, _, seq in segs:
            idx = jnp.where(r >= row0 // ts, (r - row0 // ts) % (seq // ts), idx)
        return idx

    hp = max(c for c in range(1, 9) if heads % c == 0 and blk % c == 0)
    x_spec = pl.BlockSpec((ts, hp * LANES), lambda r, h: (r, blk // hp + h))
    t_spec = pl.BlockSpec((ts, LANES), lambda r, h: (tab_block(r), 0))
    o_spec = pl.BlockSpec((ts, hp * LANES), lambda r, h: (r, h))
    specs, args = [x_spec, t_spec, t_spec, t_spec], [x, *tabs]
    if gains is not None:
        specs.append(pl.BlockSpec((hp, 1, LANES), lambda r, h: (h, 0, 0)))
        args.append(gains.reshape(heads, 1, LANES).astype(F32))
    return pl.pallas_call(
        functools.partial(_rope_body, shift=shift, rms=gains is not None, heads=hp),
        grid=(n // ts, heads // hp), in_specs=specs, out_specs=o_spec,
        out_shape=jax.ShapeDtypeStruct((n, heads * LANES), BF16),
        compiler_params=_params(("parallel", "arbitrary")), name=name)(*args)


def _rope_tables(seq):
    def table(pos, dim, theta):
        inv = theta ** (-jnp.arange(0, dim, 2, dtype=F32) / dim)
        ang = pos.astype(F32)[:, None] * inv[None, :]
        return jnp.cos(ang), jnp.sin(ang)

    pos = jnp.arange(seq)
    row = pos // GRID_W
    colp = pos % GRID_W
    one = lambda n: jnp.ones((seq, n), F32)
    zero = lambda n: jnp.zeros((seq, n), F32)
    cat = lambda *xs: jnp.concatenate(xs, -1)
    ca, sa = table(pos, PARTIAL_ROT, ROPE_THETA)
    ra = LANES - PARTIAL_ROT
    tab_a = (cat(ca, ca, one(ra)), cat(-sa, zero(LANES - 16)), cat(zero(16), sa, zero(ra)))
    cb, sb = table(pos, B_ROPE, ROPE_THETA)
    tab_b = (cat(cb, cb, one(64)), cat(-sb, zero(96)), cat(zero(32), sb, zero(64)))
    cr, sr = table(row, HEAD_DIM // 2, AXIAL_THETA)
    cc, sc = table(colp, HEAD_DIM // 2, AXIAL_THETA)
    tab_c = (cat(cr, cr, cc, cc), cat(-sr, zero(32), -sc, zero(32)), cat(zero(32), sr, zero(32), sc))
    return tab_a, tab_b, tab_c


FLASH_KEYS = 512
LOG2E = 1.4426950408889634


FLASH_STRIP = 512
ONES_ROWS = 16


def _flash_body(*refs, nq, nk, group, tq, nchunks, aliased):
    q_refs, k_refs, vt_ref = refs[:nq], refs[nq:nq + nk], refs[nq + nk]
    o_ref, qt_scr, m_scr, mblk_scr, acc_scr, s_even, s_odd = refs[nq + nk + 1 + aliased:]
    if group > 1:
        for g in range(group):
            qt_scr[:, g * tq:(g + 1) * tq] = q_refs[0][:, g * LANES:(g + 1) * LANES].astype(F32).T.astype(BF16)
    else:
        for i, r in enumerate(q_refs):
            qt_scr[i * LANES:(i + 1) * LANES, :] = r[...].astype(F32).T.astype(BF16)
    nq_all = group * tq
    ones = jnp.ones((ONES_ROWS, FLASH_KEYS), BF16)

    def keys(j):
        off = pl.multiple_of(j * FLASH_KEYS, FLASH_KEYS)
        return jnp.concatenate([r[pl.ds(off, FLASH_KEYS), :] for r in k_refs], axis=1)

    s0 = jnp.dot(keys(0), qt_scr[...], preferred_element_type=F32)
    s_even[...] = s0
    mblk_scr[...] = jnp.max(s0, 0, keepdims=True)
    m_scr[...] = jnp.full(m_scr.shape, -jnp.inf, F32)
    acc_scr[...] = jnp.zeros(acc_scr.shape, F32)

    def block(j, s_cur, s_nxt):
        k_next = keys(jnp.minimum(j + 1, nchunks - 1))
        vt1 = jnp.concatenate([vt_ref[j], ones], axis=0)
        m_prev = m_scr[...]
        m_new = jnp.maximum(m_prev, mblk_scr[...])
        alpha = jnp.exp2(m_prev - m_new)
        m_scr[...] = m_new
        for st in range(nq_all // FLASH_STRIP):
            cs = slice(st * FLASH_STRIP, (st + 1) * FLASH_STRIP)
            s_n = jnp.dot(k_next, qt_scr[:, cs], preferred_element_type=F32)
            s_nxt[:, cs] = s_n
            mblk_scr[:, cs] = jnp.max(s_n, 0, keepdims=True)
            p = jnp.exp2(s_cur[:, cs] - m_new[:, cs]).astype(BF16)
            acc_scr[:, cs] = alpha[:, cs] * acc_scr[:, cs] + jnp.dot(vt1, p, preferred_element_type=F32)

    def pair(i, carry):
        block(2 * i, s_even, s_odd)
        block(2 * i + 1, s_odd, s_even)
        return carry

    lax.fori_loop(0, nchunks // 2, pair, 0, unroll=4 if nchunks % 8 == 0 else 1)
    acc = acc_scr[...]
    o_t = acc[:LANES] / acc[LANES:LANES + 1]
    for g in range(group):
        o_ref[:, g * LANES:(g + 1) * LANES] = o_t[:, g * tq:(g + 1) * tq].T.astype(o_ref.dtype)


def _alias_previous(specs, args, prev):
    if prev is None:
        return {}
    specs.append(pl.BlockSpec(memory_space=pl.ANY))
    args.append(prev)
    return {len(args) - 1: 0}


def flash_attention(q_srcs, k_srcs, v_t, *, segs, kv_heads, group, name, rows=1024):
    n = q_srcs[0][0].shape[0]
    nq, nk = len(q_srcs), len(k_srcs)
    qw = group * LANES
    dq = LANES if group > 1 else nq * LANES
    out = None
    for row0, batch, seq in segs:
        assert seq % (2 * FLASH_KEYS) == 0 and row0 % seq == 0
        tq = _tile(seq, rows // group)
        nqb, nchunks = seq // tq, seq // FLASH_KEYS
        r = group * tq
        assert r % FLASH_STRIP == 0
        qb0, sb0 = row0 // tq, row0 // seq
        specs, args = [], []
        for arr, col, stride in q_srcs:
            assert col % qw == 0 and stride % qw == 0
            specs.append(pl.BlockSpec((tq, qw), lambda b, h, i, c=col // qw, st=stride // qw, qb0=qb0, nqb=nqb:
                                      (qb0 + b * nqb + i, c + h * st)))
            args.append(arr)
        for arr, col, stride in k_srcs:
            specs.append(pl.BlockSpec((seq, LANES), lambda b, h, i, c=col // LANES, st=stride // LANES, sb0=sb0:
                                      (sb0 + b, c + h * st)))
            args.append(arr)
        specs.append(pl.BlockSpec((nchunks, LANES, FLASH_KEYS), lambda b, h, i, sb0=sb0: (sb0 + b, h, 0)))
        args.append(v_t)
        aliases = _alias_previous(specs, args, out)
        out = pl.pallas_call(
            functools.partial(_flash_body, nq=nq, nk=nk, group=group, tq=tq, nchunks=nchunks, aliased=out is not None),
            grid=(batch, kv_heads, nqb), in_specs=specs,
            out_specs=pl.BlockSpec((tq, qw), lambda b, h, i, qb0=qb0, nqb=nqb: (qb0 + b * nqb + i, h)),
            out_shape=jax.ShapeDtypeStruct((n, kv_heads * qw), BF16),
            scratch_shapes=[pltpu.VMEM((dq, r), BF16),
                            pltpu.VMEM((1, r), F32), pltpu.VMEM((1, r), F32),
                            pltpu.VMEM((LANES + ONES_ROWS, r), F32),
                            pltpu.VMEM((FLASH_KEYS, r), F32), pltpu.VMEM((FLASH_KEYS, r), F32)],
            input_output_aliases=aliases,
            compiler_params=_params(("parallel", "arbitrary", "arbitrary")), name=name)(*args)
    return out


A_QB = 256
A_RADIUS = 64


def _band_window(seq, dilation):
    return min(A_QB + 2 * A_RADIUS * dilation, seq)


OFF_PHASE = 1e9


def _band_body(*refs, seq, dilations, scale, aliased):
    ng = len(dilations)
    q_refs, k_ref, v_ref = refs[:ng], refs[ng], refs[ng + 1]
    rel_refs = refs[ng + 2:2 * ng + 2]
    o_ref = refs[2 * ng + 2 + aliased]
    t0 = pl.program_id(2) * A_QB
    outs, lses = [], []
    for g, d in enumerate(dilations):
        w = _band_window(seq, d)
        start = pl.multiple_of(jnp.clip(t0 - A_RADIUS * d, 0, seq - w), A_RADIUS)
        kw = k_ref[pl.ds(start, w), :]
        vw = v_ref[pl.ds(start, w), :]
        s = lax.dot_general(q_refs[g][...], kw, _NT, preferred_element_type=F32) * scale
        rel = rel_refs[g][...] + (start - t0).astype(F32)
        s = jnp.where(jnp.abs(rel) <= float(A_RADIUS * d), s, MASK_VALUE)
        m = jnp.max(s, -1, keepdims=True)
        p = jnp.exp(s - m)
        den = jnp.sum(p, -1, keepdims=True)
        outs.append(jnp.dot((p / den).astype(BF16), vw, preferred_element_type=F32))
        lses.append(m + jnp.log(den))
    mx = functools.reduce(jnp.maximum, lses)
    es = [jnp.exp(l - mx) for l in lses]
    tot = functools.reduce(lambda a, b: a + b, es)
    o_ref[...] = functools.reduce(lambda a, b: a + b, [(e / tot) * o for e, o in zip(es, outs)]).astype(o_ref.dtype)


def band_attention(qk, h, *, segs):
    n = qk.shape[0]
    dilations = tuple(d for _, d in A_PATTERNS)
    out = None
    for row0, batch, seq in segs:
        assert seq % A_QB == 0 and row0 % seq == 0
        nqb = seq // A_QB
        qb0, sb0 = row0 // A_QB, row0 // seq
        specs = [pl.BlockSpec((A_QB, LANES),
                              lambda b, hh, i, g=g, qb0=qb0, nqb=nqb: (qb0 + b * nqb + i, g * A_KV_HEADS + hh))
                 for g in range(len(dilations))]
        specs.append(pl.BlockSpec((seq, LANES), lambda b, hh, i, sb0=sb0: (sb0 + b, A_Q_HEADS + hh)))
        specs.append(pl.BlockSpec((seq, LANES), lambda b, hh, i, sb0=sb0: (sb0 + b, AV_OFF // LANES + hh)))
        args = [qk] * len(dilations) + [qk, h]
        for d in dilations:
            ji = np.arange(_band_window(seq, d))[None, :] - np.arange(A_QB)[:, None]
            rel = jnp.asarray(np.where(ji % d == 0, ji, OFF_PHASE), F32)
            specs.append(_resident(rel.shape))
            args.append(rel)
        aliases = _alias_previous(specs, args, out)
        out = pl.pallas_call(
            functools.partial(_band_body, seq=seq, dilations=dilations, scale=HEAD_DIM ** -0.5,
                              aliased=out is not None),
            grid=(batch, A_KV_HEADS, nqb), in_specs=specs,
            out_specs=pl.BlockSpec((A_QB, LANES), lambda b, hh, i, qb0=qb0, nqb=nqb: (qb0 + b * nqb + i, hh)),
            out_shape=jax.ShapeDtypeStruct((n, A_KV_HEADS * LANES), BF16),
            input_output_aliases=aliases,
            compiler_params=_params(("parallel", "arbitrary", "arbitrary")), name="band_attention")(*args)
    return out


def _sigmoid(x):
    return 1.0 / (1.0 + jnp.exp(-x))


def _layer_norm(y, g, b):
    mu = jnp.mean(y, -1, keepdims=True)
    yc = y - mu
    var = jnp.mean(yc * yc, -1, keepdims=True)
    return yc * lax.rsqrt(var + LN_EPS) * g + b


def _merge_body(oa_ref, ob_ref, oc_ref, g0_ref, g1_ref, g2_ref, wa_ref, wb_ref, wc_ref, o_ref):
    ya = jnp.dot(oa_ref[...], wa_ref[...], preferred_element_type=F32)
    yb = jnp.dot(ob_ref[...], wb_ref[...], preferred_element_type=F32)
    yc = jnp.dot(oc_ref[...], wc_ref[...], preferred_element_type=F32)
    merged = (_sigmoid(g0_ref[...].astype(F32)) * ya + _sigmoid(g1_ref[...].astype(F32)) * yb
              + _sigmoid(g2_ref[...].astype(F32)) * yc)
    o_ref[...] = merged.astype(o_ref.dtype)


def merge_mixers(oa, ob, oc, h, w_o_a, w_o_b, w_o_c, *, tm=256):
    n = ob.shape[0]
    tm = _tile(n, tm)
    row = lambda w: pl.BlockSpec((tm, w), lambda i: (i, 0))
    gate = lambda j: pl.BlockSpec((tm, D_MODEL), lambda i: (i, GATE_OFF // D_MODEL + j))
    specs = ([row(oa.shape[1]), row(ob.shape[1]), row(oc.shape[1]), gate(0), gate(1), gate(2)]
             + [_resident(w_o_a.shape), _resident(w_o_b.shape), _resident(w_o_c.shape)])
    return pl.pallas_call(
        _merge_body, grid=(n // tm,), in_specs=specs, out_specs=row(D_MODEL),
        out_shape=jax.ShapeDtypeStruct((n, D_MODEL), BF16),
        compiler_params=_params(("parallel",)), name="merge_mixers")(oa, ob, oc, h, h, h, w_o_a, w_o_b, w_o_c)


def _proj_ln_body(a_ref, w_ref, x_ref, g_ref, b_ref, o_ref, ob_ref, *, alpha):
    y = alpha * x_ref[...] + jnp.dot(a_ref[...], w_ref[...], preferred_element_type=F32)
    out = _layer_norm(y, g_ref[...], b_ref[...])
    o_ref[...] = out
    ob_ref[...] = out.astype(BF16)


def proj_residual_ln(a, w, x, g, b, *, alpha, tm=256):
    n, d = x.shape
    tm = _tile(n, tm)
    row = lambda wd: pl.BlockSpec((tm, wd), lambda i: (i, 0))
    vec = pl.BlockSpec((1, d), lambda i: (0, 0))
    return pl.pallas_call(
        functools.partial(_proj_ln_body, alpha=alpha), grid=(n // tm,),
        in_specs=[row(a.shape[1]), _resident(w.shape), row(d), vec, vec],
        out_specs=[row(d), row(d)],
        out_shape=[jax.ShapeDtypeStruct((n, d), F32), jax.ShapeDtypeStruct((n, d), BF16)],
        compiler_params=_params(("parallel",)), name="proj_residual_ln")(a, w, x, g.reshape(1, d), b.reshape(1, d))


def _split_bf16(x):
    hi = x.astype(BF16)
    lo = (x - hi.astype(F32)).astype(BF16)
    return hi, lo


def _router_body(x_ref, wh_ref, wl_ref, bias_ref, idx_ref, wt_ref):
    xh, xl = _split_bf16(x_ref[...])
    nt = (((1,), (1,)), ((), ()))
    wh, wl = wh_ref[...], wl_ref[...]
    logits = (lax.dot_general(wh, xh, nt, preferred_element_type=F32)
              + (lax.dot_general(wh, xl, nt, preferred_element_type=F32)
                 + lax.dot_general(wl, xh, nt, preferred_element_type=F32)))
    scores = _sigmoid(logits)
    biased = scores + bias_ref[...]
    tn = scores.shape[1]
    gsz = N_EXPERTS // N_EXPERT_GROUPS
    neg = -jnp.inf
    sub = lax.broadcasted_iota(jnp.int32, (gsz, tn), 0).astype(F32)
    gs = []
    for g in range(N_EXPERT_GROUPS):
        blk = biased[g * gsz:(g + 1) * gsz]
        m1 = jnp.max(blk, 0, keepdims=True)
        first = jnp.min(jnp.where(blk == m1, sub, float(gsz)), 0, keepdims=True)
        m2 = jnp.max(jnp.where(sub == first, neg, blk), 0, keepdims=True)
        gs.append(m1 + m2)
    gs = jnp.concatenate(gs, axis=0)
    gi = lax.broadcasted_iota(jnp.int32, (N_EXPERT_GROUPS, tn), 0).astype(F32)
    gsel = jnp.zeros((N_EXPERT_GROUPS, tn), F32)
    for _ in range(TOPK_GROUPS):
        gm = jnp.max(gs, 0, keepdims=True)
        pick = gi == jnp.min(jnp.where(gs == gm, gi, float(N_EXPERT_GROUPS)), 0, keepdims=True)
        gsel = jnp.where(pick, 1.0, gsel)
        gs = jnp.where(pick, neg, gs)
    masked = jnp.concatenate(
        [jnp.where(gsel[g:g + 1] > 0.0, biased[g * gsz:(g + 1) * gsz], neg) for g in range(N_EXPERT_GROUPS)], axis=0)
    ei = lax.broadcasted_iota(jnp.int32, (N_EXPERTS, tn), 0).astype(F32)
    idxs, wts = [], []
    for _ in range(TOP_K):
        mx = jnp.max(masked, 0, keepdims=True)
        ix = jnp.min(jnp.where(masked == mx, ei, float(N_EXPERTS)), 0, keepdims=True)
        pick = ei == ix
        idxs.append(ix)
        wts.append(jnp.sum(jnp.where(pick, scores, 0.0), 0, keepdims=True))
        masked = jnp.where(pick, neg, masked)
    w = jnp.concatenate(wts, axis=0)
    idx_ref[...] = jnp.concatenate(idxs, axis=0).astype(jnp.int32)
    wt_ref[...] = w / jnp.sum(w, 0, keepdims=True) * ROUTED_SCALE


def router(x, w_router, router_bias, *, tn=512):
    n, d = x.shape
    tn = _tile(n, tn)
    wt = w_router.T.astype(F32)
    wh = wt.astype(BF16)
    wl = (wt - wh.astype(F32)).astype(BF16)
    full = lambda r, c: pl.BlockSpec((r, c), lambda i: (0, 0))
    out = pl.BlockSpec((TOP_K, tn), lambda i: (0, i))
    return pl.pallas_call(
        _router_body, grid=(n // tn,),
        in_specs=[pl.BlockSpec((tn, d), lambda i: (i, 0)), full(N_EXPERTS, d), full(N_EXPERTS, d), full(N_EXPERTS, 1)],
        out_specs=[out, out],
        out_shape=[jax.ShapeDtypeStruct((TOP_K, n), jnp.int32), jax.ShapeDtypeStruct((TOP_K, n), F32)],
        compiler_params=_params(("parallel",)), name="router")(x, wh, wl, router_bias.reshape(N_EXPERTS, 1).astype(F32))


def _silu(x):
    return x * _sigmoid(x)


def _experts_body(blk_e_ref, nused_ref, x_ref, wg_ref, wu_ref, wd_ref, o_ref, wgu_scr, wd_scr):
    i = pl.program_id(0)
    used = i < nused_ref[0]
    new_expert = blk_e_ref[i] != blk_e_ref[jnp.maximum(i - 1, 0)]

    @pl.when(used & ((i == 0) | new_expert))
    def _():
        wgu_scr[:, :EXPERT_FF] = wg_ref[...].astype(BF16)
        wgu_scr[:, EXPERT_FF:] = wu_ref[...].astype(BF16)
        wd_scr[...] = wd_ref[...].astype(BF16)

    @pl.when(used)
    def _():
        hgu = jnp.dot(x_ref[...], wgu_scr[...], preferred_element_type=F32)
        a = _silu(hgu[:, :EXPERT_FF]) * hgu[:, EXPERT_FF:]
        o_ref[...] = jnp.dot(a.astype(BF16), wd_scr[...], preferred_element_type=F32).astype(o_ref.dtype)

    @pl.when(jnp.logical_not(used))
    def _():
        o_ref[...] = jnp.zeros(o_ref.shape, o_ref.dtype)


def grouped_experts(xs, blk_e, nused, w_gate, w_up, w_down, layer):
    p, d = xs.shape
    nb = p // MOE_ROWS
    grid_spec = pltpu.PrefetchScalarGridSpec(
        num_scalar_prefetch=2, grid=(nb,),
        in_specs=[pl.BlockSpec((MOE_ROWS, d), lambda i, e, u: (i, 0)),
                  pl.BlockSpec((None, None, d, EXPERT_FF), lambda i, e, u: (layer, e[i], 0, 0)),
                  pl.BlockSpec((None, None, d, EXPERT_FF), lambda i, e, u: (layer, e[i], 0, 0)),
                  pl.BlockSpec((None, None, EXPERT_FF, d), lambda i, e, u: (layer, e[i], 0, 0))],
        out_specs=pl.BlockSpec((MOE_ROWS, d), lambda i, e, u: (i, 0)),
        scratch_shapes=[pltpu.VMEM((d, 2 * EXPERT_FF), BF16), pltpu.VMEM((EXPERT_FF, d), BF16)])
    return pl.pallas_call(
        _experts_body, grid_spec=grid_spec, out_shape=jax.ShapeDtypeStruct((p, d), BF16),
        compiler_params=_params(("arbitrary",)), name="grouped_experts")(blk_e, nused, xs, w_gate, w_up, w_down)


def _shared_ln_body(*refs, alpha, nout):
    x_ref, xb_ref, yg_ref, gw_ref, wgu_ref, wd_ref, g_ref, b_ref = refs[:8]
    outs = refs[len(refs) - nout:]
    o_ref, ob_ref = outs[0], outs[1:]
    hgu = jnp.dot(xb_ref[...], wgu_ref[...], preferred_element_type=F32)
    a = _silu(hgu[:, :SHARED_FF]) * hgu[:, SHARED_FF:]
    y = jnp.dot(a.astype(BF16), wd_ref[...], preferred_element_type=F32)
    gw = gw_ref[...]
    for k in range(TOP_K):
        y = y + yg_ref[k].astype(F32) * gw[:, k:k + 1]
    out = _layer_norm(alpha * x_ref[...] + y, g_ref[...], b_ref[...])
    o_ref[...] = out
    for ref in ob_ref:
        ref[...] = out.astype(BF16)


def shared_expert_ln(x, xb, yg, gate_w, w_gu, w_d, g, b, *, alpha, row0, out_row0, out_rows, prev, tm=256):
    d = x.shape[1]
    n = yg.shape[1]
    tm = _tile(n, tm)
    assert row0 % tm == 0 and out_row0 % tm == 0
    src = pl.BlockSpec((tm, d), lambda i: (row0 // tm + i, 0))
    yg_spec = pl.BlockSpec((TOP_K, tm, d), lambda i: (0, i, 0))
    gw_spec = pl.BlockSpec((tm, TOP_K), lambda i: (i, 0))
    dst = pl.BlockSpec((tm, d), lambda i: (out_row0 // tm + i, 0))
    vec = pl.BlockSpec((1, d), lambda i: (0, 0))
    out_shape = [jax.ShapeDtypeStruct((out_rows, d), dt) for dt in (F32, BF16)[:len(prev)]]
    specs = [src, src, yg_spec, gw_spec, _resident(w_gu.shape), _resident(w_d.shape), vec, vec]
    args = [x, xb, yg, gate_w, w_gu, w_d, g.reshape(1, d), b.reshape(1, d)]
    aliases = {}
    for o, buf in enumerate(prev):
        if buf is not None:
            specs.append(pl.BlockSpec(memory_space=pl.ANY))
            args.append(buf)
            aliases[len(args) - 1] = o
    return pl.pallas_call(
        functools.partial(_shared_ln_body, alpha=alpha, nout=len(prev)), grid=(n // tm,),
        in_specs=specs, out_specs=[dst] * len(out_shape), out_shape=out_shape, input_output_aliases=aliases,
        compiler_params=_params(("parallel",)), name="shared_expert_ln")(*args)


def _dispatch_plan(idx, n):
    a = n * TOP_K
    flat_e = idx.T.reshape(a)
    iota = jnp.arange(a, dtype=jnp.int32)
    _, order = lax.sort_key_val(flat_e, iota)
    _, rank = lax.sort_key_val(order, iota)
    experts = jnp.arange(N_EXPERTS, dtype=jnp.int32)
    counts = jnp.sum((flat_e[:, None] == experts[None, :]).astype(jnp.int32), axis=0)
    padded = (counts + MOE_ROWS - 1) // MOE_ROWS * MOE_ROWS
    pad_end = jnp.cumsum(padded)
    pad_start = pad_end - padded
    start = jnp.cumsum(counts) - counts
    n_blocks = a // MOE_ROWS + N_EXPERTS
    blk0 = jnp.arange(n_blocks, dtype=jnp.int32) * MOE_ROWS
    blk_e = jnp.minimum(jnp.sum((pad_end[None, :] <= blk0[:, None]).astype(jnp.int32), axis=1), N_EXPERTS - 1)
    src = (start - pad_start)[blk_e][:, None] + blk0[:, None] + jnp.arange(MOE_ROWS, dtype=jnp.int32)[None, :]
    slot_token = order[jnp.clip(src, 0, a - 1).reshape(-1)] // TOP_K
    nused = (pad_end[-1] // MOE_ROWS).astype(jnp.int32).reshape(1)
    slot_of = ((pad_start - start)[flat_e] + rank).reshape(n, TOP_K)
    return slot_token, blk_e.astype(jnp.int32), nused, slot_of


def _prep_layer(l, p):
    bf = lambda a: a.astype(BF16)
    w_in = p['w_in'][l]
    offs = np.cumsum((0, A_Q_HEADS * HEAD_DIM, A_KV_HEADS * HEAD_DIM, A_KV_HEADS * HEAD_DIM, B_Q_RANK, B_KV_RANK,
                      B_ROPE, C_Q_HEADS * HEAD_DIM, C_KV_HEADS * HEAD_DIM, C_KV_HEADS * HEAD_DIM, N_BRANCH * D_MODEL))
    piece = lambda i: w_in[:, offs[i]:offs[i + 1]]
    zeros = lambda c: jnp.zeros((D_MODEL, c), w_in.dtype)
    w_h = jnp.concatenate([piece(9), piece(0), piece(1), piece(2), piece(3), piece(4), piece(5), zeros(LANES - B_ROPE),
                           piece(6), piece(7), piece(8), zeros(H_COLS - H_USED)], axis=1)
    w_uq = p['w_uq'][l].reshape(B_Q_RANK, B_HEADS, B_NOPE + B_ROPE)
    w_uq_rope = jnp.pad(w_uq[:, :, B_NOPE:], ((0, 0), (0, 0), (0, LANES - B_ROPE)))
    w_q = jnp.concatenate([w_uq[:, :, :B_NOPE].reshape(B_Q_RANK, -1), w_uq_rope.reshape(B_Q_RANK, -1)], axis=1)
    w_ukv = p['w_ukv'][l].reshape(B_KV_RANK, B_HEADS, B_NOPE + B_V)
    w_k = w_ukv[:, :, :B_NOPE].reshape(B_KV_RANK, -1)
    w_v_t = w_ukv[:, :, B_NOPE:].reshape(B_KV_RANK, -1).T
    return dict(
        w_h=bf(w_h), w_q=bf(w_q), w_k=bf(w_k), w_v_t=bf(w_v_t), w_cv_t=bf(piece(8).T),
        w_o_a=bf(p['w_o_a'][l]), w_o_b=bf(p['w_o_b'][l]), w_o_c=bf(p['w_o_c'][l]), w_out=bf(p['w_out'][l]),
        w_s_gu=bf(jnp.concatenate([p['w_s_gate'][l], p['w_s_up'][l]], axis=-1)), w_s_d=bf(p['w_s_down'][l]),
        c_gains=jnp.concatenate([jnp.tile(p['c_q_norm'][l][None] * (HEAD_DIM ** -0.5 * LOG2E), (C_Q_HEADS, 1)),
                                 jnp.tile(p['c_k_norm'][l][None], (C_KV_HEADS, 1))], axis=0),
    )


def _token_mixer(x, xb, lw, l, p, tabs, segs, alpha):
    tab_a, tab_b, tab_c = tabs
    h = matmul(xb, lw['w_h'], name="in_proj")
    assert all((window // dilation) // 2 == A_RADIUS for window, dilation in A_PATTERNS)
    qk_a = rope(h, tab_a, name="rope_a", segs=segs, col=AQ_OFF, heads=A_Q_HEADS + A_KV_HEADS, shift=PARTIAL_ROT // 2)
    o_a = band_attention(qk_a, h, segs=segs)
    q_b = matmul(h, lw['w_q'], name="mla_q", x_col=BCQ_OFF, gain=p['mla_q_norm'][l],
                 out_scale=(B_NOPE + B_ROPE) ** -0.5 * LOG2E)
    k_b = matmul(h, lw['w_k'], name="mla_k", x_col=BCKV_OFF, gain=p['mla_kv_norm'][l])
    vt_b = matmul_t(h, lw['w_v_t'], name="mla_v_t", tm=FLASH_KEYS, x_col=BCKV_OFF, gain=p['mla_kv_norm'][l])
    q_b_rope = rope(q_b, tab_b, name="rope_bq", segs=segs, col=B_HEADS * LANES, heads=B_HEADS, shift=B_ROPE // 2)
    k_b_rope = rope(h, tab_b, name="rope_bk", segs=segs, col=BKR_OFF, heads=1, shift=B_ROPE // 2)
    o_b = flash_attention([(q_b, 0, LANES), (q_b_rope, 0, LANES)], [(k_b, 0, LANES), (k_b_rope, 0, 0)], vt_b,
                          segs=segs, kv_heads=B_HEADS, group=1, name="flash_b")
    qk_c = rope(h, tab_c, name="rope_c", segs=segs, col=CQ_OFF, heads=C_Q_HEADS + C_KV_HEADS, shift=HEAD_DIM // 4,
                gains=lw['c_gains'])
    vt_c = matmul_t(xb, lw['w_cv_t'], name="c_v_t", tm=FLASH_KEYS)
    group = C_Q_HEADS // C_KV_HEADS
    o_c = flash_attention([(qk_c, 0, group * LANES)], [(qk_c, C_Q_HEADS * LANES, LANES)], vt_c,
                          segs=segs, kv_heads=C_KV_HEADS, group=group, name="flash_c")
    merged = merge_mixers(o_a, o_b, o_c, h, lw['w_o_a'], lw['w_o_b'], lw['w_o_c'])
    return proj_residual_ln(merged, lw['w_out'], x, p['ln1_g'][l], p['ln1_b'][l], alpha=alpha)


MOE_CHUNK = 16384


def _moe(x, xb, lw, l, p, alpha, segs, final):
    n, d = x.shape
    idx, wts = router(x, p['w_router'][l], p['router_bias'][l])
    gate_w = wts.T
    if final:
        bufs = [(None,) for _ in segs]
    else:
        bufs = [(None, None)]
    for si, (seg0, batch, seq) in sorted(enumerate(segs), key=lambda e: e[1][1] * e[1][2]):
        rows = batch * seq
        step = min(MOE_CHUNK, rows)
        assert rows % step == 0
        for r0 in range(seg0, seg0 + rows, step):
            slot_token, blk_e, nused, slot_of = _dispatch_plan(idx[:, r0:r0 + step], step)
            xs = xb.at[slot_token + r0].get(mode='promise_in_bounds')
            y = grouped_experts(xs, blk_e, nused, p['w_e_gate'], p['w_e_up'], p['w_e_down'], l)
            yg = y.at[slot_of.T].get(mode='promise_in_bounds')
            b = si if final else 0
            bufs[b] = tuple(shared_expert_ln(
                x, xb, yg, gate_w[r0:r0 + step], lw['w_s_gu'], lw['w_s_d'], p['ln2_g'][l], p['ln2_b'][l], alpha=alpha,
                row0=r0, out_row0=r0 - seg0 if final else r0, out_rows=rows if final else n, prev=bufs[b]))
    return tuple(b[0] for b in bufs) if final else bufs[0]


def _run_trunk(xs3, layers, p):
    d = xs3[0].shape[-1]
    segs, row0 = [], 0
    for x3 in xs3:
        segs.append((row0, x3.shape[0], x3.shape[1]))
        row0 += x3.shape[0] * x3.shape[1]
    segs = tuple(segs)
    alpha = (2 * len(layers)) ** 0.25
    tabs = _rope_tables(max(seq for _, _, seq in segs))
    x = jnp.concatenate([x3.reshape(-1, d) for x3 in xs3], axis=0)
    xb = x.astype(BF16)
    last = len(layers) - 1
    for l, lw in enumerate(layers):
        x, xb = _token_mixer(x, xb, lw, l, p, tabs, segs, alpha)
        if l < last:
            x, xb = _moe(x, xb, lw, l, p, alpha, segs, final=False)
    outs = _moe(x, xb, layers[last], last, p, alpha, segs, final=True)
    return tuple(o.reshape(b, s, d) for o, (_, b, s) in zip(outs, segs))


def kernel(x_prompt, x_sample, ln1_g, ln1_b, w_in, mla_q_norm, w_uq, mla_kv_norm, w_ukv, c_q_norm, c_k_norm, w_o_a, w_o_b, w_o_c, w_out, ln2_g, ln2_b, w_router, router_bias, w_e_gate, w_e_up, w_e_down, w_s_gate, w_s_up, w_s_down):
    p = dict(ln1_g=ln1_g, ln1_b=ln1_b, w_in=w_in, mla_q_norm=mla_q_norm, w_uq=w_uq, mla_kv_norm=mla_kv_norm,
             w_ukv=w_ukv, c_q_norm=c_q_norm, c_k_norm=c_k_norm, w_o_a=w_o_a, w_o_b=w_o_b, w_o_c=w_o_c, w_out=w_out,
             ln2_g=ln2_g, ln2_b=ln2_b, w_router=w_router, router_bias=router_bias, w_e_gate=w_e_gate,
             w_e_up=w_e_up, w_e_down=w_e_down, w_s_gate=w_s_gate, w_s_up=w_s_up, w_s_down=w_s_down)
    layers = [_prep_layer(l, p) for l in range(w_in.shape[0])]
    return _run_trunk((x_prompt, x_sample), layers, p)
```

```python
import functools

import numpy as np
import jax
import jax.numpy as jnp
from jax import lax
from jax.experimental import pallas as pl
from jax.experimental.pallas import tpu as pltpu

F32 = jnp.float32
BF16 = jnp.bfloat16

D_MODEL = 2048
HEAD_DIM = 128
A_PATTERNS = ((128, 1), (512, 4), (2048, 16))
A_KV_HEADS = 4
A_Q_HEADS = len(A_PATTERNS) * A_KV_HEADS
ROPE_THETA = 500000.0
PARTIAL_ROT = HEAD_DIM // 4
B_HEADS = 8
B_Q_RANK = 512
B_KV_RANK = 256
B_NOPE = 128
B_ROPE = 64
B_V = 128
C_Q_HEADS = 8
C_KV_HEADS = 2
AXIAL_THETA = 10000.0
GRID_W = 64
N_BRANCH = 3
N_EXPERTS = 64
TOP_K = 8
N_EXPERT_GROUPS = 8
TOPK_GROUPS = 4
EXPERT_FF = 512
SHARED_FF = 512
ROUTED_SCALE = 2.5
LN_EPS = 1e-5
RMS_EPS = 1e-6
MASK_VALUE = -1e30

LANES = 128
VMEM_LIMIT = 56 * 1024 * 1024
_NT = (((1,), (1,)), ((), ()))

GATE_OFF = 0
AQ_OFF = GATE_OFF + N_BRANCH * D_MODEL
AK_OFF = AQ_OFF + A_Q_HEADS * HEAD_DIM
AV_OFF = AK_OFF + A_KV_HEADS * HEAD_DIM
BCQ_OFF = AV_OFF + A_KV_HEADS * HEAD_DIM
BCKV_OFF = BCQ_OFF + B_Q_RANK
BKR_OFF = BCKV_OFF + B_KV_RANK
CQ_OFF = BKR_OFF + LANES
CK_OFF = CQ_OFF + C_Q_HEADS * HEAD_DIM
CV_OFF = CK_OFF + C_KV_HEADS * HEAD_DIM
H_USED = CV_OFF + C_KV_HEADS * HEAD_DIM
H_COLS = -(-H_USED // 1024) * 1024

MOE_ROWS = 256


def _tile(n, t):
    t = min(n, t)
    assert n % t == 0, (n, t)
    return t


def _params(sem):
    return pltpu.CompilerParams(dimension_semantics=sem, vmem_limit_bytes=VMEM_LIMIT)


def _resident(shape):
    nd = len(shape)
    return pl.BlockSpec(shape, lambda *_: (0,) * nd, pipeline_mode=pl.Buffered(1))


def _mm_body(x_ref, w_ref, o_ref):
    o_ref[...] = jnp.dot(x_ref[...], w_ref[...], preferred_element_type=F32).astype(o_ref.dtype)


def _mm_rms_body(x_ref, g_ref, w_ref, o_ref, *, out_scale):
    xf = x_ref[...].astype(F32)
    y = xf * lax.rsqrt(jnp.mean(xf * xf, -1, keepdims=True) + RMS_EPS) * g_ref[...]
    acc = jnp.dot(y.astype(BF16), w_ref[...], preferred_element_type=F32)
    o_ref[...] = (acc * out_scale).astype(o_ref.dtype)


def matmul(x, w, *, name, x_col=0, gain=None, out_scale=1.0, tm=1024, tn=1024, out_dtype=BF16):
    m = x.shape[0]
    k, n = w.shape
    tm, tn = _tile(m, tm), _tile(n, tn)
    assert x_col % k == 0
    x_spec = pl.BlockSpec((tm, k), lambda i, j: (i, x_col // k))
    w_spec = pl.BlockSpec((k, tn), lambda i, j: (0, j))
    o_spec = pl.BlockSpec((tm, tn), lambda i, j: (i, j))
    if gain is None:
        assert out_scale == 1.0
        body, specs, args = _mm_body, [x_spec, w_spec], (x, w)
    else:
        g_spec = pl.BlockSpec((1, k), lambda i, j: (0, 0))
        body = functools.partial(_mm_rms_body, out_scale=out_scale)
        specs, args = [x_spec, g_spec, w_spec], (x, gain.reshape(1, k).astype(F32), w)
    return pl.pallas_call(
        body, grid=(m // tm, n // tn), in_specs=specs, out_specs=o_spec,
        out_shape=jax.ShapeDtypeStruct((m, n), out_dtype),
        compiler_params=_params(("parallel", "arbitrary")), name=name)(*args)


def _mm_t_body(x_ref, w_ref, o_ref):
    o_ref[...] = lax.dot_general(w_ref[...], x_ref[...], _NT, preferred_element_type=F32).astype(o_ref.dtype)


def _mm_t_rms_body(x_ref, g_ref, w_ref, o_ref):
    xf = x_ref[...].astype(F32)
    y = xf * lax.rsqrt(jnp.mean(xf * xf, -1, keepdims=True) + RMS_EPS) * g_ref[...]
    o_ref[...] = lax.dot_general(w_ref[...], y.astype(BF16), _NT, preferred_element_type=F32).astype(o_ref.dtype)


def matmul_t(x, w_t, *, name, tm, x_col=0, gain=None):
    m = x.shape[0]
    n, k = w_t.shape
    assert m % tm == 0 and x_col % k == 0
    x_spec = pl.BlockSpec((tm, k), lambda i: (i, x_col // k))
    o_spec = pl.BlockSpec((None, n, tm), lambda i: (i, 0, 0))
    if gain is None:
        body, specs, args = _mm_t_body, [x_spec, _resident(w_t.shape)], (x, w_t)
    else:
        g_spec = pl.BlockSpec((1, k), lambda i: (0, 0))
        body, specs, args = _mm_t_rms_body, [x_spec, g_spec, _resident(w_t.shape)], (x, gain.reshape(1, k).astype(F32), w_t)
    return pl.pallas_call(
        body, grid=(m // tm,), in_specs=specs, out_specs=o_spec,
        out_shape=jax.ShapeDtypeStruct((m // tm, n, tm), BF16),
        compiler_params=_params(("parallel",)), name=name)(*args)


def _rope_body(*refs, shift, rms, heads):
    if rms:
        x_ref, c_ref, s1_ref, s2_ref, g_ref, o_ref = refs
    else:
        x_ref, c_ref, s1_ref, s2_ref, o_ref = refs
    c, s1, s2 = c_ref[...], s1_ref[...], s2_ref[...]
    for hd in range(heads):
        sl = slice(hd * LANES, (hd + 1) * LANES)
        x = x_ref[:, sl].astype(F32)
        if rms:
            x = x * lax.rsqrt(jnp.mean(x * x, -1, keepdims=True) + RMS_EPS) * g_ref[hd]
        out = x * c + pltpu.roll(x, LANES - shift, 1) * s1 + pltpu.roll(x, shift, 1) * s2
        o_ref[:, sl] = out.astype(o_ref.dtype)


def rope(x, tabs, *, name, segs, col, heads, shift, gains=None, ts=1024):
    n = x.shape[0]
    ts = min([ts] + [seq for _, _, seq in segs])
    assert n % ts == 0 and all(row0 % ts == 0 and seq % ts == 0 for row0, _, seq in segs)
    blk = col // LANES

    def tab_block(r):
        idx = r * 0
        for row0, _, seq in segs:
            idx = jnp.where(r >= row0 // ts, (r - row0 // ts) % (seq // ts), idx)
        return idx

    hp = max(c for c in range(1, 9) if heads % c == 0 and blk % c == 0)
    x_spec = pl.BlockSpec((ts, hp * LANES), lambda r, h: (r, blk // hp + h))
    t_spec = pl.BlockSpec((ts, LANES), lambda r, h: (tab_block(r), 0))
    o_spec = pl.BlockSpec((ts, hp * LANES), lambda r, h: (r, h))
    specs, args = [x_spec, t_spec, t_spec, t_spec], [x, *tabs]
    if gains is not None:
        specs.append(pl.BlockSpec((hp, 1, LANES), lambda r, h: (h, 0, 0)))
        args.append(gains.reshape(heads, 1, LANES).astype(F32))
    return pl.pallas_call(
        functools.partial(_rope_body, shift=shift, rms=gains is not None, heads=hp),
        grid=(n // ts, heads // hp), in_specs=specs, out_specs=o_spec,
        out_shape=jax.ShapeDtypeStruct((n, heads * LANES), BF16),
        compiler_params=_params(("parallel", "arbitrary")), name=name)(*args)


def _rope_tables(seq):
    def table(pos, dim, theta):
        inv = theta ** (-jnp.arange(0, dim, 2, dtype=F32) / dim)
        ang = pos.astype(F32)[:, None] * inv[None, :]
        return jnp.cos(ang), jnp.sin(ang)

    pos = jnp.arange(seq)
    row = pos // GRID_W
    colp = pos % GRID_W
    one = lambda n: jnp.ones((seq, n), F32)
    zero = lambda n: jnp.zeros((seq, n), F32)
    cat = lambda *xs: jnp.concatenate(xs, -1)
    ca, sa = table(pos, PARTIAL_ROT, ROPE_THETA)
    ra = LANES - PARTIAL_ROT
    tab_a = (cat(ca, ca, one(ra)), cat(-sa, zero(LANES - 16)), cat(zero(16), sa, zero(ra)))
    cb, sb = table(pos, B_ROPE, ROPE_THETA)
    tab_b = (cat(cb, cb, one(64)), cat(-sb, zero(96)), cat(zero(32), sb, zero(64)))
    cr, sr = table(row, HEAD_DIM // 2, AXIAL_THETA)
    cc, sc = table(colp, HEAD_DIM // 2, AXIAL_THETA)
    tab_c = (cat(cr, cr, cc, cc), cat(-sr, zero(32), -sc, zero(32)), cat(zero(32), sr, zero(32), sc))
    return tab_a, tab_b, tab_c


FLASH_KEYS = 512
LOG2E = 1.4426950408889634


FLASH_STRIP = 512
ONES_ROWS = 16


def _flash_body(*refs, nq, nk, group, tq, nchunks, aliased):
    q_refs, k_refs, vt_ref = refs[:nq], refs[nq:nq + nk], refs[nq + nk]
    o_ref, qt_scr, m_scr, mblk_scr, acc_scr, s_even, s_odd = refs[nq + nk + 1 + aliased:]
    if group > 1:
        for g in range(group):
            qt_scr[:, g * tq:(g + 1) * tq] = q_refs[0][:, g * LANES:(g + 1) * LANES].astype(F32).T.astype(BF16)
    else:
        for i, r in enumerate(q_refs):
            qt_scr[i * LANES:(i + 1) * LANES, :] = r[...].astype(F32).T.astype(BF16)
    nq_all = group * tq
    ones = jnp.ones((ONES_ROWS, FLASH_KEYS), BF16)

    def keys(j):
        off = pl.multiple_of(j * FLASH_KEYS, FLASH_KEYS)
        return jnp.concatenate([r[pl.ds(off, FLASH_KEYS), :] for r in k_refs], axis=1)

    s0 = jnp.dot(keys(0), qt_scr[...], preferred_element_type=F32)
    s_even[...] = s0
    mblk_scr[...] = jnp.max(s0, 0, keepdims=True)
    m_scr[...] = jnp.full(m_scr.shape, -jnp.inf, F32)
    acc_scr[...] = jnp.zeros(acc_scr.shape, F32)

    def block(j, s_cur, s_nxt):
        k_next = keys(jnp.minimum(j + 1, nchunks - 1))
        vt1 = jnp.concatenate([vt_ref[j], ones], axis=0)
        m_prev = m_scr[...]
        m_new = jnp.maximum(m_prev, mblk_scr[...])
        alpha = jnp.exp2(m_prev - m_new)
        m_scr[...] = m_new
        for st in range(nq_all // FLASH_STRIP):
            cs = slice(st * FLASH_STRIP, (st + 1) * FLASH_STRIP)
            s_n = jnp.dot(k_next, qt_scr[:, cs], preferred_element_type=F32)
            s_nxt[:, cs] = s_n
            mblk_scr[:, cs] = jnp.max(s_n, 0, keepdims=True)
            p = jnp.exp2(s_cur[:, cs] - m_new[:, cs]).astype(BF16)
            acc_scr[:, cs] = alpha[:, cs] * acc_scr[:, cs] + jnp.dot(vt1, p, preferred_element_type=F32)

    def pair(i, carry):
        block(2 * i, s_even, s_odd)
        block(2 * i + 1, s_odd, s_even)
        return carry

    lax.fori_loop(0, nchunks // 2, pair, 0, unroll=4 if nchunks % 8 == 0 else 1)
    acc = acc_scr[...]
    o_t = acc[:LANES] / acc[LANES:LANES + 1]
    for g in range(group):
        o_ref[:, g * LANES:(g + 1) * LANES] = o_t[:, g * tq:(g + 1) * tq].T.astype(o_ref.dtype)


def _alias_previous(specs, args, prev):
    if prev is None:
        return {}
    specs.append(pl.BlockSpec(memory_space=pl.ANY))
    args.append(prev)
    return {len(args) - 1: 0}


def flash_attention(q_srcs, k_srcs, v_t, *, segs, kv_heads, group, name, rows=1024):
    n = q_srcs[0][0].shape[0]
    nq, nk = len(q_srcs), len(k_srcs)
    qw = group * LANES
    dq = LANES if group > 1 else nq * LANES
    out = None
    for row0, batch, seq in segs:
        assert seq % (2 * FLASH_KEYS) == 0 and row0 % seq == 0
        tq = _tile(seq, rows // group)
        nqb, nchunks = seq // tq, seq // FLASH_KEYS
        r = group * tq
        assert r % FLASH_STRIP == 0
        qb0, sb0 = row0 // tq, row0 // seq
        specs, args = [], []
        for arr, col, stride in q_srcs:
            assert col % qw == 0 and stride % qw == 0
            specs.append(pl.BlockSpec((tq, qw), lambda b, h, i, c=col // qw, st=stride // qw, qb0=qb0, nqb=nqb:
                                      (qb0 + b * nqb + i, c + h * st)))
            args.append(arr)
        for arr, col, stride in k_srcs:
            specs.append(pl.BlockSpec((seq, LANES), lambda b, h, i, c=col // LANES, st=stride // LANES, sb0=sb0:
                                      (sb0 + b, c + h * st)))
            args.append(arr)
        specs.append(pl.BlockSpec((nchunks, LANES, FLASH_KEYS), lambda b, h, i, sb0=sb0: (sb0 + b, h, 0)))
        args.append(v_t)
        aliases = _alias_previous(specs, args, out)
        out = pl.pallas_call(
            functools.partial(_flash_body, nq=nq, nk=nk, group=group, tq=tq, nchunks=nchunks, aliased=out is not None),
            grid=(batch, kv_heads, nqb), in_specs=specs,
            out_specs=pl.BlockSpec((tq, qw), lambda b, h, i, qb0=qb0, nqb=nqb: (qb0 + b * nqb + i, h)),
            out_shape=jax.ShapeDtypeStruct((n, kv_heads * qw), BF16),
            scratch_shapes=[pltpu.VMEM((dq, r), BF16),
                            pltpu.VMEM((1, r), F32), pltpu.VMEM((1, r), F32),
                            pltpu.VMEM((LANES + ONES_ROWS, r), F32),
                            pltpu.VMEM((FLASH_KEYS, r), F32), pltpu.VMEM((FLASH_KEYS, r), F32)],
            input_output_aliases=aliases,
            compiler_params=_params(("parallel", "arbitrary", "arbitrary")), name=name)(*args)
    return out


A_QB = 256
A_RADIUS = 64


def _band_window(seq, dilation):
    return min(A_QB + 2 * A_RADIUS * dilation, seq)


OFF_PHASE = 1e9


def _band_body(*refs, seq, dilations, scale, aliased):
    ng = len(dilations)
    q_refs, k_ref, v_ref = refs[:ng], refs[ng], refs[ng + 1]
    rel_refs = refs[ng + 2:2 * ng + 2]
    o_ref = refs[2 * ng + 2 + aliased]
    t0 = pl.program_id(2) * A_QB
    outs, lses = [], []
    for g, d in enumerate(dilations):
        w = _band_window(seq, d)
        start = pl.multiple_of(jnp.clip(t0 - A_RADIUS * d, 0, seq - w), A_RADIUS)
        kw = k_ref[pl.ds(start, w), :]
        vw = v_ref[pl.ds(start, w), :]
        s = lax.dot_general(q_refs[g][...], kw, _NT, preferred_element_type=F32)
        rel = rel_refs[g][...] + (start - t0).astype(F32)
        s = jnp.where(jnp.abs(rel) <= float(A_RADIUS * d), s, MASK_VALUE)
        m = jnp.max(s, -1, keepdims=True)
        p = jnp.exp2((s - m) * (scale * LOG2E))
        den = jnp.sum(p, -1, keepdims=True)
        outs.append(jnp.dot(p.astype(BF16), vw, preferred_element_type=F32) / den)
        lses.append(m * scale + jnp.log(den))
    mx = functools.reduce(jnp.maximum, lses)
    es = [jnp.exp(l - mx) for l in lses]
    tot = functools.reduce(lambda a, b: a + b, es)
    o_ref[...] = functools.reduce(lambda a, b: a + b, [(e / tot) * o for e, o in zip(es, outs)]).astype(o_ref.dtype)


def band_attention(qk, h, *, segs):
    n = qk.shape[0]
    dilations = tuple(d for _, d in A_PATTERNS)
    out = None
    for row0, batch, seq in segs:
        assert seq % A_QB == 0 and row0 % seq == 0
        nqb = seq // A_QB
        qb0, sb0 = row0 // A_QB, row0 // seq
        specs = [pl.BlockSpec((A_QB, LANES),
                              lambda b, hh, i, g=g, qb0=qb0, nqb=nqb: (qb0 + b * nqb + i, g * A_KV_HEADS + hh))
                 for g in range(len(dilations))]
        specs.append(pl.BlockSpec((seq, LANES), lambda b, hh, i, sb0=sb0: (sb0 + b, A_Q_HEADS + hh)))
        specs.append(pl.BlockSpec((seq, LANES), lambda b, hh, i, sb0=sb0: (sb0 + b, AV_OFF // LANES + hh)))
        args = [qk] * len(dilations) + [qk, h]
        for d in dilations:
            ji = np.arange(_band_window(seq, d))[None, :] - np.arange(A_QB)[:, None]
            rel = jnp.asarray(np.where(ji % d == 0, ji, OFF_PHASE), F32)
            specs.append(_resident(rel.shape))
            args.append(rel)
        aliases = _alias_previous(specs, args, out)
        out = pl.pallas_call(
            functools.partial(_band_body, seq=seq, dilations=dilations, scale=HEAD_DIM ** -0.5,
                              aliased=out is not None),
            grid=(batch, A_KV_HEADS, nqb), in_specs=specs,
            out_specs=pl.BlockSpec((A_QB, LANES), lambda b, hh, i, qb0=qb0, nqb=nqb: (qb0 + b * nqb + i, hh)),
            out_shape=jax.ShapeDtypeStruct((n, A_KV_HEADS * LANES), BF16),
            input_output_aliases=aliases,
            compiler_params=_params(("parallel", "arbitrary", "arbitrary")), name="band_attention")(*args)
    return out


def _sigmoid(x):
    return 1.0 / (1.0 + jnp.exp(-x))


def _layer_norm(y, g, b):
    mu = jnp.mean(y, -1, keepdims=True)
    yc = y - mu
    var = jnp.mean(yc * yc, -1, keepdims=True)
    return yc * lax.rsqrt(var + LN_EPS) * g + b


def _merge_body(oa_ref, ob_ref, oc_ref, g0_ref, g1_ref, g2_ref, wa_ref, wb_ref, wc_ref, o_ref):
    ya = jnp.dot(oa_ref[...], wa_ref[...], preferred_element_type=F32)
    yb = jnp.dot(ob_ref[...], wb_ref[...], preferred_element_type=F32)
    yc = jnp.dot(oc_ref[...], wc_ref[...], preferred_element_type=F32)
    merged = (_sigmoid(g0_ref[...].astype(F32)) * ya + _sigmoid(g1_ref[...].astype(F32)) * yb
              + _sigmoid(g2_ref[...].astype(F32)) * yc)
    o_ref[...] = merged.astype(o_ref.dtype)


def merge_mixers(oa, ob, oc, h, w_o_a, w_o_b, w_o_c, *, tm=256):
    n = ob.shape[0]
    tm = _tile(n, tm)
    row = lambda w: pl.BlockSpec((tm, w), lambda i: (i, 0))
    gate = lambda j: pl.BlockSpec((tm, D_MODEL), lambda i: (i, GATE_OFF // D_MODEL + j))
    specs = ([row(oa.shape[1]), row(ob.shape[1]), row(oc.shape[1]), gate(0), gate(1), gate(2)]
             + [_resident(w_o_a.shape), _resident(w_o_b.shape), _resident(w_o_c.shape)])
    return pl.pallas_call(
        _merge_body, grid=(n // tm,), in_specs=specs, out_specs=row(D_MODEL),
        out_shape=jax.ShapeDtypeStruct((n, D_MODEL), BF16),
        compiler_params=_params(("parallel",)), name="merge_mixers")(oa, ob, oc, h, h, h, w_o_a, w_o_b, w_o_c)


def _proj_ln_body(a_ref, w_ref, x_ref, g_ref, b_ref, o_ref, ob_ref, *, alpha):
    y = alpha * x_ref[...] + jnp.dot(a_ref[...], w_ref[...], preferred_element_type=F32)
    out = _layer_norm(y, g_ref[...], b_ref[...])
    o_ref[...] = out
    ob_ref[...] = out.astype(BF16)


def proj_residual_ln(a, w, x, g, b, *, alpha, tm=256):
    n, d = x.shape
    tm = _tile(n, tm)
    row = lambda wd: pl.BlockSpec((tm, wd), lambda i: (i, 0))
    vec = pl.BlockSpec((1, d), lambda i: (0, 0))
    return pl.pallas_call(
        functools.partial(_proj_ln_body, alpha=alpha), grid=(n // tm,),
        in_specs=[row(a.shape[1]), _resident(w.shape), row(d), vec, vec],
        out_specs=[row(d), row(d)],
        out_shape=[jax.ShapeDtypeStruct((n, d), F32), jax.ShapeDtypeStruct((n, d), BF16)],
        compiler_params=_params(("parallel",)), name="proj_residual_ln")(a, w, x, g.reshape(1, d), b.reshape(1, d))


def _split_bf16(x):
    hi = x.astype(BF16)
    lo = (x - hi.astype(F32)).astype(BF16)
    return hi, lo


def _router_body(x_ref, wh_ref, wl_ref, bias_ref, idx_ref, wt_ref):
    xh, xl = _split_bf16(x_ref[...])
    nt = (((1,), (1,)), ((), ()))
    wh, wl = wh_ref[...], wl_ref[...]
    logits = (lax.dot_general(wh, xh, nt, preferred_element_type=F32)
              + (lax.dot_general(wh, xl, nt, preferred_element_type=F32)
                 + lax.dot_general(wl, xh, nt, preferred_element_type=F32)))
    scores = _sigmoid(logits)
    biased = scores + bias_ref[...]
    tn = scores.shape[1]
    gsz = N_EXPERTS // N_EXPERT_GROUPS
    neg = -jnp.inf
    sub = lax.broadcasted_iota(jnp.int32, (gsz, tn), 0).astype(F32)
    gs = []
    for g in range(N_EXPERT_GROUPS):
        blk = biased[g * gsz:(g + 1) * gsz]
        m1 = jnp.max(blk, 0, keepdims=True)
        first = jnp.min(jnp.where(blk == m1, sub, float(gsz)), 0, keepdims=True)
        m2 = jnp.max(jnp.where(sub == first, neg, blk), 0, keepdims=True)
        gs.append(m1 + m2)
    gs = jnp.concatenate(gs, axis=0)
    gi = lax.broadcasted_iota(jnp.int32, (N_EXPERT_GROUPS, tn), 0).astype(F32)
    gsel = jnp.zeros((N_EXPERT_GROUPS, tn), F32)
    for _ in range(TOPK_GROUPS):
        gm = jnp.max(gs, 0, keepdims=True)
        pick = gi == jnp.min(jnp.where(gs == gm, gi, float(N_EXPERT_GROUPS)), 0, keepdims=True)
        gsel = jnp.where(pick, 1.0, gsel)
        gs = jnp.where(pick, neg, gs)
    masked = jnp.concatenate(
        [jnp.where(gsel[g:g + 1] > 0.0, biased[g * gsz:(g + 1) * gsz], neg) for g in range(N_EXPERT_GROUPS)], axis=0)
    ei = lax.broadcasted_iota(jnp.int32, (N_EXPERTS, tn), 0).astype(F32)
    idxs, wts = [], []
    for _ in range(TOP_K):
        mx = jnp.max(masked, 0, keepdims=True)
        ix = jnp.min(jnp.where(masked == mx, ei, float(N_EXPERTS)), 0, keepdims=True)
        pick = ei == ix
        idxs.append(ix)
        wts.append(jnp.sum(jnp.where(pick, scores, 0.0), 0, keepdims=True))
        masked = jnp.where(pick, neg, masked)
    w = jnp.concatenate(wts, axis=0)
    idx_ref[...] = jnp.concatenate(idxs, axis=0).astype(jnp.int32)
    wt_ref[...] = w / jnp.sum(w, 0, keepdims=True) * ROUTED_SCALE


def router(x, w_router, router_bias, *, tn=512):
    n, d = x.shape
    tn = _tile(n, tn)
    wt = w_router.T.astype(F32)
    wh = wt.astype(BF16)
    wl = (wt - wh.astype(F32)).astype(BF16)
    full = lambda r, c: pl.BlockSpec((r, c), lambda i: (0, 0))
    out = pl.BlockSpec((TOP_K, tn), lambda i: (0, i))
    return pl.pallas_call(
        _router_body, grid=(n // tn,),
        in_specs=[pl.BlockSpec((tn, d), lambda i: (i, 0)), full(N_EXPERTS, d), full(N_EXPERTS, d), full(N_EXPERTS, 1)],
        out_specs=[out, out],
        out_shape=[jax.ShapeDtypeStruct((TOP_K, n), jnp.int32), jax.ShapeDtypeStruct((TOP_K, n), F32)],
        compiler_params=_params(("parallel",)), name="router")(x, wh, wl, router_bias.reshape(N_EXPERTS, 1).astype(F32))


def _silu(x):
    return x * _sigmoid(x)


def _experts_body(blk_e_ref, nused_ref, x_ref, wg_ref, wu_ref, wd_ref, o_ref, wgu_scr, wd_scr):
    i = pl.program_id(0)
    used = i < nused_ref[0]
    new_expert = blk_e_ref[i] != blk_e_ref[jnp.maximum(i - 1, 0)]

    @pl.when(used & ((i == 0) | new_expert))
    def _():
        wgu_scr[:, :EXPERT_FF] = wg_ref[...].astype(BF16)
        wgu_scr[:, EXPERT_FF:] = wu_ref[...].astype(BF16)
        wd_scr[...] = wd_ref[...].astype(BF16)

    @pl.when(used)
    def _():
        hgu = jnp.dot(x_ref[...], wgu_scr[...], preferred_element_type=F32)
        a = _silu(hgu[:, :EXPERT_FF]) * hgu[:, EXPERT_FF:]
        o_ref[...] = jnp.dot(a.astype(BF16), wd_scr[...], preferred_element_type=F32).astype(o_ref.dtype)

    @pl.when(jnp.logical_not(used))
    def _():
        o_ref[...] = jnp.zeros(o_ref.shape, o_ref.dtype)


def grouped_experts(xs, blk_e, nused, w_gate, w_up, w_down, layer):
    p, d = xs.shape
    nb = p // MOE_ROWS
    grid_spec = pltpu.PrefetchScalarGridSpec(
        num_scalar_prefetch=2, grid=(nb,),
        in_specs=[pl.BlockSpec((MOE_ROWS, d), lambda i, e, u: (i, 0)),
                  pl.BlockSpec((None, None, d, EXPERT_FF), lambda i, e, u: (layer, e[i], 0, 0)),
                  pl.BlockSpec((None, None, d, EXPERT_FF), lambda i, e, u: (layer, e[i], 0, 0)),
                  pl.BlockSpec((None, None, EXPERT_FF, d), lambda i, e, u: (layer, e[i], 0, 0))],
        out_specs=pl.BlockSpec((MOE_ROWS, d), lambda i, e, u: (i, 0)),
        scratch_shapes=[pltpu.VMEM((d, 2 * EXPERT_FF), BF16), pltpu.VMEM((EXPERT_FF, d), BF16)])
    return pl.pallas_call(
        _experts_body, grid_spec=grid_spec, out_shape=jax.ShapeDtypeStruct((p, d), BF16),
        compiler_params=_params(("arbitrary",)), name="grouped_experts")(blk_e, nused, xs, w_gate, w_up, w_down)


def _shared_ln_body(*refs, alpha, nout):
    x_ref, xb_ref, yg_ref, gw_ref, wgu_ref, wd_ref, g_ref, b_ref = refs[:8]
    outs = refs[len(refs) - nout:]
    o_ref, ob_ref = outs[0], outs[1:]
    hgu = jnp.dot(xb_ref[...], wgu_ref[...], preferred_element_type=F32)
    a = _silu(hgu[:, :SHARED_FF]) * hgu[:, SHARED_FF:]
    y = jnp.dot(a.astype(BF16), wd_ref[...], preferred_element_type=F32)
    gw = gw_ref[...]
    for k in range(TOP_K):
        y = y + yg_ref[k].astype(F32) * gw[:, k:k + 1]
    out = _layer_norm(alpha * x_ref[...] + y, g_ref[...], b_ref[...])
    o_ref[...] = out
    for ref in ob_ref:
        ref[...] = out.astype(BF16)


def shared_expert_ln(x, xb, yg, gate_w, w_gu, w_d, g, b, *, alpha, row0, out_row0, out_rows, prev, tm=256):
    d = x.shape[1]
    n = yg.shape[1]
    tm = _tile(n, tm)
    assert row0 % tm == 0 and out_row0 % tm == 0
    src = pl.BlockSpec((tm, d), lambda i: (row0 // tm + i, 0))
    yg_spec = pl.BlockSpec((TOP_K, tm, d), lambda i: (0, i, 0))
    gw_spec = pl.BlockSpec((tm, TOP_K), lambda i: (i, 0))
    dst = pl.BlockSpec((tm, d), lambda i: (out_row0 // tm + i, 0))
    vec = pl.BlockSpec((1, d), lambda i: (0, 0))
    out_shape = [jax.ShapeDtypeStruct((out_rows, d), dt) for dt in (F32, BF16)[:len(prev)]]
    specs = [src, src, yg_spec, gw_spec, _resident(w_gu.shape), _resident(w_d.shape), vec, vec]
    args = [x, xb, yg, gate_w, w_gu, w_d, g.reshape(1, d), b.reshape(1, d)]
    aliases = {}
    for o, buf in enumerate(prev):
        if buf is not None:
            specs.append(pl.BlockSpec(memory_space=pl.ANY))
            args.append(buf)
            aliases[len(args) - 1] = o
    return pl.pallas_call(
        functools.partial(_shared_ln_body, alpha=alpha, nout=len(prev)), grid=(n // tm,),
        in_specs=specs, out_specs=[dst] * len(out_shape), out_shape=out_shape, input_output_aliases=aliases,
        compiler_params=_params(("parallel",)), name="shared_expert_ln")(*args)


def _dispatch_plan(idx, n):
    a = n * TOP_K
    flat_e = idx.T.reshape(a)
    iota = jnp.arange(a, dtype=jnp.int32)
    _, order = lax.sort_key_val(flat_e, iota)
    _, rank = lax.sort_key_val(order, iota)
    experts = jnp.arange(N_EXPERTS, dtype=jnp.int32)
    counts = jnp.sum((flat_e[:, None] == experts[None, :]).astype(jnp.int32), axis=0)
    padded = (counts + MOE_ROWS - 1) // MOE_ROWS * MOE_ROWS
    pad_end = jnp.cumsum(padded)
    pad_start = pad_end - padded
    start = jnp.cumsum(counts) - counts
    n_blocks = a // MOE_ROWS + N_EXPERTS
    blk0 = jnp.arange(n_blocks, dtype=jnp.int32) * MOE_ROWS
    blk_e = jnp.minimum(jnp.sum((pad_end[None, :] <= blk0[:, None]).astype(jnp.int32), axis=1), N_EXPERTS - 1)
    src = (start - pad_start)[blk_e][:, None] + blk0[:, None] + jnp.arange(MOE_ROWS, dtype=jnp.int32)[None, :]
    slot_token = order[jnp.clip(src, 0, a - 1).reshape(-1)] // TOP_K
    nused = (pad_end[-1] // MOE_ROWS).astype(jnp.int32).reshape(1)
    slot_of = ((pad_start - start)[flat_e] + rank).reshape(n, TOP_K)
    return slot_token, blk_e.astype(jnp.int32), nused, slot_of


def _prep_layer(l, p):
    bf = lambda a: a.astype(BF16)
    w_in = p['w_in'][l]
    offs = np.cumsum((0, A_Q_HEADS * HEAD_DIM, A_KV_HEADS * HEAD_DIM, A_KV_HEADS * HEAD_DIM, B_Q_RANK, B_KV_RANK,
                      B_ROPE, C_Q_HEADS * HEAD_DIM, C_KV_HEADS * HEAD_DIM, C_KV_HEADS * HEAD_DIM, N_BRANCH * D_MODEL))
    piece = lambda i: w_in[:, offs[i]:offs[i + 1]]
    zeros = lambda c: jnp.zeros((D_MODEL, c), w_in.dtype)
    w_h = jnp.concatenate([piece(9), piece(0), piece(1), piece(2), piece(3), piece(4), piece(5), zeros(LANES - B_ROPE),
                           piece(6), piece(7), piece(8), zeros(H_COLS - H_USED)], axis=1)
    w_uq = p['w_uq'][l].reshape(B_Q_RANK, B_HEADS, B_NOPE + B_ROPE)
    w_uq_rope = jnp.pad(w_uq[:, :, B_NOPE:], ((0, 0), (0, 0), (0, LANES - B_ROPE)))
    w_q = jnp.concatenate([w_uq[:, :, :B_NOPE].reshape(B_Q_RANK, -1), w_uq_rope.reshape(B_Q_RANK, -1)], axis=1)
    w_ukv = p['w_ukv'][l].reshape(B_KV_RANK, B_HEADS, B_NOPE + B_V)
    w_k = w_ukv[:, :, :B_NOPE].reshape(B_KV_RANK, -1)
    w_v_t = w_ukv[:, :, B_NOPE:].reshape(B_KV_RANK, -1).T
    return dict(
        w_h=bf(w_h), w_q=bf(w_q), w_k=bf(w_k), w_v_t=bf(w_v_t), w_cv_t=bf(piece(8).T),
        w_o_a=bf(p['w_o_a'][l]), w_o_b=bf(p['w_o_b'][l]), w_o_c=bf(p['w_o_c'][l]), w_out=bf(p['w_out'][l]),
        w_s_gu=bf(jnp.concatenate([p['w_s_gate'][l], p['w_s_up'][l]], axis=-1)), w_s_d=bf(p['w_s_down'][l]),
        c_gains=jnp.concatenate([jnp.tile(p['c_q_norm'][l][None] * (HEAD_DIM ** -0.5 * LOG2E), (C_Q_HEADS, 1)),
                                 jnp.tile(p['c_k_norm'][l][None], (C_KV_HEADS, 1))], axis=0),
    )


def _token_mixer(x, xb, lw, l, p, tabs, segs, alpha):
    tab_a, tab_b, tab_c = tabs
    h = matmul(xb, lw['w_h'], name="in_proj")
    assert all((window // dilation) // 2 == A_RADIUS for window, dilation in A_PATTERNS)
    qk_a = rope(h, tab_a, name="rope_a", segs=segs, col=AQ_OFF, heads=A_Q_HEADS + A_KV_HEADS, shift=PARTIAL_ROT // 2)
    o_a = band_attention(qk_a, h, segs=segs)
    q_b = matmul(h, lw['w_q'], name="mla_q", x_col=BCQ_OFF, gain=p['mla_q_norm'][l],
                 out_scale=(B_NOPE + B_ROPE) ** -0.5 * LOG2E)
    k_b = matmul(h, lw['w_k'], name="mla_k", x_col=BCKV_OFF, gain=p['mla_kv_norm'][l])
    vt_b = matmul_t(h, lw['w_v_t'], name="mla_v_t", tm=FLASH_KEYS, x_col=BCKV_OFF, gain=p['mla_kv_norm'][l])
    q_b_rope = rope(q_b, tab_b, name="rope_bq", segs=segs, col=B_HEADS * LANES, heads=B_HEADS, shift=B_ROPE // 2)
    k_b_rope = rope(h, tab_b, name="rope_bk", segs=segs, col=BKR_OFF, heads=1, shift=B_ROPE // 2)
    o_b = flash_attention([(q_b, 0, LANES), (q_b_rope, 0, LANES)], [(k_b, 0, LANES), (k_b_rope, 0, 0)], vt_b,
                          segs=segs, kv_heads=B_HEADS, group=1, name="flash_b")
    qk_c = rope(h, tab_c, name="rope_c", segs=segs, col=CQ_OFF, heads=C_Q_HEADS + C_KV_HEADS, shift=HEAD_DIM // 4,
                gains=lw['c_gains'])
    vt_c = matmul_t(xb, lw['w_cv_t'], name="c_v_t", tm=FLASH_KEYS)
    group = C_Q_HEADS // C_KV_HEADS
    o_c = flash_attention([(qk_c, 0, group * LANES)], [(qk_c, C_Q_HEADS * LANES, LANES)], vt_c,
                          segs=segs, kv_heads=C_KV_HEADS, group=group, name="flash_c")
    merged = merge_mixers(o_a, o_b, o_c, h, lw['w_o_a'], lw['w_o_b'], lw['w_o_c'])
    return proj_residual_ln(merged, lw['w_out'], x, p['ln1_g'][l], p['ln1_b'][l], alpha=alpha)


MOE_CHUNK = 16384


def _moe(x, xb, lw, l, p, alpha, segs, final):
    n, d = x.shape
    idx, wts = router(x, p['w_router'][l], p['router_bias'][l])
    gate_w = wts.T
    if final:
        bufs = [(None,) for _ in segs]
    else:
        bufs = [(None, None)]
    for si, (seg0, batch, seq) in sorted(enumerate(segs), key=lambda e: e[1][1] * e[1][2]):
        rows = batch * seq
        step = min(MOE_CHUNK, rows)
        assert rows % step == 0
        for r0 in range(seg0, seg0 + rows, step):
            slot_token, blk_e, nused, slot_of = _dispatch_plan(idx[:, r0:r0 + step], step)
            xs = xb.at[slot_token + r0].get(mode='promise_in_bounds')
            y = grouped_experts(xs, blk_e, nused, p['w_e_gate'], p['w_e_up'], p['w_e_down'], l)
            yg = y.at[slot_of.T].get(mode='promise_in_bounds')
            b = si if final else 0
            bufs[b] = tuple(shared_expert_ln(
                x, xb, yg, gate_w[r0:r0 + step], lw['w_s_gu'], lw['w_s_d'], p['ln2_g'][l], p['ln2_b'][l], alpha=alpha,
                row0=r0, out_row0=r0 - seg0 if final else r0, out_rows=rows if final else n, prev=bufs[b]))
    return tuple(b[0] for b in bufs) if final else bufs[0]


def _run_trunk(xs3, layers, p):
    d = xs3[0].shape[-1]
    segs, row0 = [], 0
    for x3 in xs3:
        segs.append((row0, x3.shape[0], x3.shape[1]))
        row0 += x3.shape[0] * x3.shape[1]
    segs = tuple(segs)
    alpha = (2 * len(layers)) ** 0.25
    tabs = _rope_tables(max(seq for _, _, seq in segs))
    x = jnp.concatenate([x3.reshape(-1, d) for x3 in xs3], axis=0)
    xb = x.astype(BF16)
    last = len(layers) - 1
    for l, lw in enumerate(layers):
        x, xb = _token_mixer(x, xb, lw, l, p, tabs, segs, alpha)
        if l < last:
            x, xb = _moe(x, xb, lw, l, p, alpha, segs, final=False)
    outs = _moe(x, xb, layers[last], last, p, alpha, segs, final=True)
    return tuple(o.reshape(b, s, d) for o, (_, b, s) in zip(outs, segs))


def kernel(x_prompt, x_sample, ln1_g, ln1_b, w_in, mla_q_norm, w_uq, mla_kv_norm, w_ukv, c_q_norm, c_k_norm, w_o_a, w_o_b, w_o_c, w_out, ln2_g, ln2_b, w_router, router_bias, w_e_gate, w_e_up, w_e_down, w_s_gate, w_s_up, w_s_down):
    p = dict(ln1_g=ln1_g, ln1_b=ln1_b, w_in=w_in, mla_q_norm=mla_q_norm, w_uq=w_uq, mla_kv_norm=mla_kv_norm,
             w_ukv=w_ukv, c_q_norm=c_q_norm, c_k_norm=c_k_norm, w_o_a=w_o_a, w_o_b=w_o_b, w_o_c=w_o_c, w_out=w_out,
             ln2_g=ln2_g, ln2_b=ln2_b, w_router=w_router, router_bias=router_bias, w_e_gate=w_e_gate,
             w_e_up=w_e_up, w_e_down=w_e_down, w_s_gate=w_s_gate, w_s_up=w_s_up, w_s_down=w_s_down)
    layers = [_prep_layer(l, p) for l in range(w_in.shape[0])]
    return _run_trunk((x_prompt, x_sample), layers, p)
```
